```python
import math
import jax, jax.numpy as jnp
from jax import lax
import numpy as np

D_MODEL = 1024
BATCH = 1
SEQ = 16384
DEPTH = 2
DEC_BATCH = 32
DEC_SEQ = 4
PAST_LEN = 16384
PAGE_SIZE = 128

HEAD_DIM = 64
D_ATT = D_MODEL // 2
N_ATT_HEADS = D_ATT // HEAD_DIM
D_RNN = D_MODEL - D_ATT
N_RNN_BLOCKS = 8
RNN_BLOCK = D_RNN // N_RNN_BLOCKS
CONV_WIDTH = 4
LRU_C = 8.0
DILATED_PATTERNS = ((128, 1), (512, 4), (2048, 16))
MAX_WINDOW = max(w for w, _ in DILATED_PATTERNS)
BAND = DILATED_PATTERNS[0][0] // DILATED_PATTERNS[0][1]
ROPE_THETA = 500000.0
ROT_DIM = HEAD_DIM // 4
D_FF = -(-8 * D_MODEL // (3 * 256)) * 256
D_IN = 3 * D_ATT + 2 * D_RNN
RMS_EPS = 1e-6

kernel_name = "hymba_rglru_dilated_swa_decode_step"

F32 = jnp.float32


def _rmsnorm(x, g):
    x32 = x.astype(F32)
    y = x32 * lax.rsqrt(jnp.mean(x32 * x32, axis=-1, keepdims=True) + RMS_EPS)
    return (y * g.astype(F32)).astype(x.dtype)


def _rope_partial(x, pos):
    half = ROT_DIM // 2
    inv = ROPE_THETA ** (-jnp.arange(half, dtype=F32) * 2.0 / ROT_DIM)
    ang = pos.astype(F32)[:, None] * inv[None, :]
    cos = jnp.cos(ang)[None, :, None, :]
    sin = jnp.sin(ang)[None, :, None, :]
    xr = x[..., :ROT_DIM].astype(F32)
    x1, x2 = xr[..., :half], xr[..., half:]
    rot = jnp.concatenate([x1 * cos - x2 * sin, x2 * cos + x1 * sin], axis=-1)
    return jnp.concatenate([rot.astype(x.dtype), x[..., ROT_DIM:]], axis=-1)


def _masked_softmax(s, valid):
    s = jnp.where(valid, s, -jnp.inf)
    m = jnp.max(s, axis=-1, keepdims=True)
    e = jnp.exp(s - m)
    den = jnp.sum(e, axis=-1, keepdims=True)
    return e / den, m[..., 0] + jnp.log(den[..., 0])


def _dilated_band_prompt(q, k, v, dil):
    B, S, H, E = q.shape
    span = dil * BAND
    Sp = -(-S // span) * span
    nb = Sp // span

    def blocks(t):
        t = jnp.pad(t, ((0, 0), (0, Sp - S), (0, 0), (0, 0)))
        return t.reshape(B, nb, BAND, dil, H, E)

    def with_prev(t):
        prev = jnp.pad(t, ((0, 0), (1, 0), (0, 0), (0, 0), (0, 0), (0, 0)))[:, :-1]
        return jnp.concatenate([prev, t], axis=2)

    qb = blocks(q)
    kk, vv = with_prev(blocks(k)), with_prev(blocks(v))
    s = jnp.einsum('bjarhe,bjcrhe->bjrhac', qb, kk) * (1.0 / math.sqrt(E))
    a = jnp.arange(BAND)[:, None]
    c = jnp.arange(2 * BAND)[None, :]
    dist = BAND + a - c
    band_ok = (dist >= 0) & (dist <= BAND)
    j = jnp.arange(nb)[:, None, None]
    valid = band_ok[None] & ((j > 0) | (c >= BAND)[None])
    p, lse = _masked_softmax(s, valid[None, :, None, None])
    o = jnp.einsum('bjrhac,bjcrhe->bjarhe', p, vv).reshape(B, Sp, H, E)[:, :S]
    lse = lse.transpose(0, 1, 4, 2, 3).reshape(B, Sp, H)[:, :S]
    return o, lse


def _dilated_gather_sample(q, kcat, vcat, dil, w_buf):
    B, T, H, E = q.shape
    i = jnp.arange(T)[:, None]
    mm = jnp.arange(BAND + 1)[None, :]
    idx = w_buf + i - dil * mm
    valid = idx >= 0
    idx = jnp.maximum(idx, 0)
    kg = kcat[:, idx]
    vg = vcat[:, idx]
    s = jnp.einsum('bthe,btmhe->bhtm', q, kg) * (1.0 / math.sqrt(E))
    p, lse = _masked_softmax(s, valid[None, None])
    o = jnp.einsum('bhtm,btmhe->bthe', p, vg)
    return o, lse.transpose(0, 2, 1)


def _merge_by_denominator(outs, lses):
    w = jax.nn.softmax(jnp.stack(lses, axis=0), axis=0)
    return jnp.sum(w[..., None] * jnp.stack(outs, axis=0), axis=0)


def _rglru(xc, h0, w_ga, b_ga, w_gx, b_gx, lam):
    B, T, _ = xc.shape
    x32 = xc.astype(F32)
    xb = x32.reshape(B, T, N_RNN_BLOCKS, RNN_BLOCK)
    r = jax.nn.sigmoid(jnp.einsum('btnc,ncd->btnd', xb, w_ga.astype(F32)).reshape(B, T, D_RNN) + b_ga.astype(F32))
    ig = jax.nn.sigmoid(jnp.einsum('btnc,ncd->btnd', xb, w_gx.astype(F32)).reshape(B, T, D_RNN) + b_gx.astype(F32))
    log_a = -LRU_C * r * jax.nn.softplus(-lam.astype(F32))
    a = jnp.exp(log_a)
    b = jnp.sqrt(-jnp.expm1(2.0 * log_a)) * (ig * x32)

    def step(h, ab):
        at, bt = ab
        h = at * h + bt
        return h, h

    h_last, hs = lax.scan(step, h0.astype(F32), (a.swapaxes(0, 1), b.swapaxes(0, 1)))
    return hs.swapaxes(0, 1), h_last


def _layer(x, pos, cache_k, cache_v, conv_state, h0,
           g1, w_in, conv_w, conv_b, w_ga, b_ga, w_gx, b_gx, lam, w_out,
           g2, w_fg, w_fu, w_fd, prompt):
    B, T, _ = x.shape
    hn = _rmsnorm(x, g1)
    z = hn @ w_in
    q, k, v, xr, yg = jnp.split(z, [D_ATT, 2 * D_ATT, 3 * D_ATT, 3 * D_ATT + D_RNN], axis=-1)
    q = _rope_partial(q.reshape(B, T, N_ATT_HEADS, HEAD_DIM), pos)
    k = _rope_partial(k.reshape(B, T, N_ATT_HEADS, HEAD_DIM), pos)
    v = v.reshape(B, T, N_ATT_HEADS, HEAD_DIM)
    q32 = q.astype(F32)
    if prompt:
        res = [_dilated_band_prompt(q32, k.astype(F32), v.astype(F32), d) for _, d in DILATED_PATTERNS]
        keep = min(MAX_WINDOW, T)
        new_k, new_v = k[:, T - keep:], v[:, T - keep:]
    else:
        w_buf = cache_k.shape[1]
        kcat = jnp.concatenate([cache_k, k], axis=1)
        vcat = jnp.concatenate([cache_v, v], axis=1)
        res = [_dilated_gather_sample(q32, kcat.astype(F32), vcat.astype(F32), d, w_buf) for _, d in DILATED_PATTERNS]
        new_k, new_v = kcat[:, T:], vcat[:, T:]
    att = _merge_by_denominator([o for o, _ in res], [l for _, l in res])
    att = att.astype(x.dtype).reshape(B, T, D_ATT)
    xpad = jnp.concatenate([conv_state, xr], axis=1)
    xc = sum(xpad[:, j:j + T] * conv_w[j] for j in range(CONV_WIDTH)) + conv_b
    new_conv = xpad[:, T:]
    hs, h_last = _rglru(xc, h0, w_ga, b_ga, w_gx, b_gx, lam)
    rnn = hs.astype(x.dtype) * jax.nn.gelu(yg, approximate=True)
    x = x + jnp.concatenate([att, rnn], axis=-1) @ w_out
    hf = _rmsnorm(x, g2)
    x = x + (jax.nn.silu(hf @ w_fg) * (hf @ w_fu)) @ w_fd
    return x, new_k, new_v, new_conv, h_last.astype(x.dtype)


def setup_inputs(seed: int = 0) -> dict:
    key = jax.random.key(seed)
    ks = jax.random.split(key, 24)
    w_buf = min(MAX_WINDOW, PAST_LEN)
    nrm = lambda k, shape, s: jax.random.normal(k, shape, F32) * s
    a0 = jax.random.uniform(ks[14], (DEPTH, D_RNN), F32, minval=0.9, maxval=0.999)
    sa = a0 ** (1.0 / LRU_C)
    return {
        "x_prompt": nrm(ks[0], (BATCH, SEQ, D_MODEL), 1.0),
        "x_sample": nrm(ks[1], (DEC_BATCH, DEC_SEQ, D_MODEL), 1.0),
        "cache_k": nrm(ks[2], (DEPTH, DEC_BATCH, w_buf, N_ATT_HEADS, HEAD_DIM), 1.0),
        "cache_v": nrm(ks[3], (DEPTH, DEC_BATCH, w_buf, N_ATT_HEADS, HEAD_DIM), 1.0),
        "state_conv": nrm(ks[4], (DEPTH, DEC_BATCH, CONV_WIDTH - 1, D_RNN), 1.0),
        "state_h": nrm(ks[5], (DEPTH, DEC_BATCH, D_RNN), 0.5),
        "norm1_g": 1.0 + nrm(ks[6], (DEPTH, D_MODEL), 0.01),
        "w_in": nrm(ks[7], (DEPTH, D_MODEL, D_IN), D_MODEL ** -0.5),
        "conv_w": nrm(ks[8], (DEPTH, CONV_WIDTH, D_RNN), CONV_WIDTH ** -0.5),
        "conv_b": nrm(ks[9], (DEPTH, D_RNN), 0.01),
        "w_gate_a": nrm(ks[10], (DEPTH, N_RNN_BLOCKS, RNN_BLOCK, RNN_BLOCK), RNN_BLOCK ** -0.5),
        "b_gate_a": nrm(ks[11], (DEPTH, D_RNN), 0.01),
        "w_gate_x": nrm(ks[12], (DEPTH, N_RNN_BLOCKS, RNN_BLOCK, RNN_BLOCK), RNN_BLOCK ** -0.5),
        "b_gate_x": nrm(ks[13], (DEPTH, D_RNN), 0.01),
        "lru_lambda": jnp.log(sa) - jnp.log1p(-sa),
        "w_out": nrm(ks[15], (DEPTH, D_MODEL, D_MODEL), D_MODEL ** -0.5),
        "norm2_g": 1.0 + nrm(ks[16], (DEPTH, D_MODEL), 0.01),
        "w_ffn_gate": nrm(ks[17], (DEPTH, D_MODEL, D_FF), D_MODEL ** -0.5),
        "w_ffn_up": nrm(ks[18], (DEPTH, D_MODEL, D_FF), D_MODEL ** -0.5),
        "w_ffn_down": nrm(ks[19], (DEPTH, D_FF, D_MODEL), D_FF ** -0.5),
        "final_norm_g": 1.0 + nrm(ks[20], (D_MODEL,), 0.01),
    }


def reference(x_prompt, x_sample, cache_k, cache_v, state_conv, state_h,
              norm1_g, w_in, conv_w, conv_b, w_gate_a, b_gate_a, w_gate_x, b_gate_x,
              lru_lambda, w_out, norm2_g, w_ffn_gate, w_ffn_up, w_ffn_down, final_norm_g):
    B, S, _ = x_prompt.shape
    T = x_sample.shape[1]
    pos_p = jnp.arange(S, dtype=jnp.int32)
    pos_s = PAST_LEN + jnp.arange(T, dtype=jnp.int32)
    xp, xs = x_prompt, x_sample
    pk, pv, pc, ph, sk, sv, sc, sh = [], [], [], [], [], [], [], []
    for l in range(DEPTH):
        w = (norm1_g[l], w_in[l], conv_w[l], conv_b[l], w_gate_a[l], b_gate_a[l],
             w_gate_x[l], b_gate_x[l], lru_lambda[l], w_out[l], norm2_g[l],
             w_ffn_gate[l], w_ffn_up[l], w_ffn_down[l])
        conv0 = jnp.zeros((B, CONV_WIDTH - 1, D_RNN), xp.dtype)
        h0 = jnp.zeros((B, D_RNN), xp.dtype)
        xp, k1, v1, c1, h1 = _layer(xp, pos_p, None, None, conv0, h0, *w, prompt=True)
        xs, k2, v2, c2, h2 = _layer(xs, pos_s, cache_k[l], cache_v[l], state_conv[l], state_h[l], *w, prompt=False)
        pk.append(k1); pv.append(v1); pc.append(c1); ph.append(h1)
        sk.append(k2); sv.append(v2); sc.append(c2); sh.append(h2)
    y_prompt = _rmsnorm(xp, final_norm_g)
    y_sample = _rmsnorm(xs, final_norm_g)
    return (y_prompt, y_sample,
            jnp.stack(pk), jnp.stack(pv), jnp.stack(pc), jnp.stack(ph),
            jnp.stack(sk), jnp.stack(sv), jnp.stack(sc), jnp.stack(sh))
```

```python
import functools
import math

import numpy as np
import jax
import jax.numpy as jnp
from jax import lax
from jax.experimental import pallas as pl
from jax.experimental.pallas import tpu as pltpu

F32 = jnp.float32
BF16 = jnp.bfloat16

D_MODEL = 1024
HEAD_DIM = 64
D_ATT = 512
N_HEADS = 8
D_RNN = 512
N_RNN_BLOCKS = 8
CONV_WIDTH = 4
LRU_C = 8.0
DILATIONS = (1, 4, 16)
BAND = 128
MAX_WINDOW = 2048
ROPE_THETA = 500000.0
ROT_DIM = 16
D_FF = 2816
D_IN = 3 * D_ATT + 2 * D_RNN
RMS_EPS = 1e-6
PAST_LEN = 16384

LANES = 128
SUBLANES = 8
N_SLABS = D_ATT // LANES
FF_CHUNK = 256
NEG_BIG = -1e30
VMEM_LIMIT = 56 * 1024 * 1024

TM_FRONT = 256
TM_FFN = 512
ATT_BLOCK = MAX_WINDOW
CACHE_HEADS = 4


def _dot(a, b):
    return jnp.dot(a, b, preferred_element_type=F32)


def _dot_nt(a, b):
    return lax.dot_general(a, b, (((1,), (1,)), ((), ())), preferred_element_type=F32)


def _rmsnorm(x, g):
    return x * lax.rsqrt(jnp.mean(x * x, axis=-1, keepdims=True) + RMS_EPS) * g


def _softplus(x):
    return jnp.maximum(x, 0.0) + jnp.log1p(jnp.exp(-jnp.abs(x)))


def _gelu_tanh(x):
    return 0.5 * x * (1.0 + jnp.tanh(math.sqrt(2.0 / math.pi) * (x + 0.044715 * (x * x * x))))


def _rope_slab(x, c, s1, s2):
    return x * c + pltpu.roll(x, 8, 1) * s1 + pltpu.roll(x, LANES - 8, 1) * s2


def _lru_coeffs(xc, wg_ref, bga, bgx, lam):
    xb = xc.astype(BF16)
    half = D_RNN // 2
    g0 = _dot(xb[:, :half], wg_ref[0])
    g1 = _dot(xb[:, half:], wg_ref[1])
    r = jax.nn.sigmoid(jnp.concatenate([g0[:, :half], g1[:, :half]], axis=1) + bga)
    ig = jax.nn.sigmoid(jnp.concatenate([g0[:, half:], g1[:, half:]], axis=1) + bgx)
    log_a = (-LRU_C) * r * _softplus(-lam)
    a = jnp.exp(log_a)
    th = jnp.tanh(log_a)
    b = jnp.sqrt(-2.0 * th / (1.0 - th)) * (ig * xc)
    return a, b


def _front_kernel(x_ref, g1_ref, w_ref, cos_ref, s1_ref, s2_ref, cw_ref, cb_ref, wg_ref,
                  bga_ref, bgx_ref, lam_ref,
                  qkv1_ref, qkv4_ref, qkv16_ref, rnn_ref, klast_ref, vlast_ref, tail_ref, hlast_ref,
                  zs_s, p4_s, xr_s, a_s, b_s, hs_s, h_s, *, tm):
    i = pl.program_id(0)
    hn = _rmsnorm(x_ref[...], g1_ref[...]).astype(BF16)

    cos = cos_ref[...]
    s1 = s1_ref[...]
    s2 = s2_ref[...]
    for part in range(3):
        z = _dot(hn, w_ref[:, part * D_ATT:(part + 1) * D_ATT])
        for c in range(N_SLABS):
            zc = z[:, c * LANES:(c + 1) * LANES]
            if part < 2:
                zc = _rope_slab(zc, cos, s1, s2)
            j = part * N_SLABS + c
            if part == 1:
                klast_ref[:, c * LANES:(c + 1) * LANES] = zc
            if part == 2:
                vlast_ref[:, c * LANES:(c + 1) * LANES] = zc
            if part == 0:
                zc = zc * (1.0 / math.sqrt(HEAD_DIM))
            zs_s[j] = zc
            qkv1_ref[:, j * LANES:(j + 1) * LANES] = zc.astype(BF16)

    n4 = tm // 4
    n16 = tm // 16
    for j in range(3 * N_SLABS):
        for c4 in range(4):
            rows = zs_s[j, pl.ds(c4, n4, stride=4), :]
            p4_s[j, c4 * n4:(c4 + 1) * n4, :] = rows
            qkv4_ref[0, c4, :, j * LANES:(j + 1) * LANES] = rows.astype(BF16)
        for c4 in range(4):
            for jj in range(4):
                rows = p4_s[j, pl.ds(c4 * n4 + jj, n16, stride=4), :]
                qkv16_ref[0, c4 + 4 * jj, :, j * LANES:(j + 1) * LANES] = rows.astype(BF16)

    xr = _dot(hn, w_ref[:, 3 * D_ATT:3 * D_ATT + D_RNN])
    yg = _dot(hn, w_ref[:, 3 * D_ATT + D_RNN:])

    @pl.when(i == 0)
    def _():
        xr_s[0:SUBLANES, :] = jnp.zeros((SUBLANES, D_RNN), F32)
        h_s[...] = jnp.zeros_like(h_s)

    @pl.when(i > 0)
    def _():
        xr_s[0:SUBLANES, :] = xr_s[tm:tm + SUBLANES, :]

    xr_s[SUBLANES:tm + SUBLANES, :] = xr
    tail_ref[...] = xr[tm - SUBLANES:, :]
    xc = xr * cw_ref[3:4, :] + cb_ref[...]
    for k in range(1, CONV_WIDTH):
        xc = xc + xr_s[pl.ds(SUBLANES - k, tm), :] * cw_ref[3 - k:4 - k, :]
    a, b = _lru_coeffs(xc, wg_ref, bga_ref[...], bgx_ref[...], lam_ref[...])
    a_s[...] = a
    b_s[...] = b

    sub = lax.broadcasted_iota(jnp.int32, (SUBLANES, D_RNN), 0)

    def group(g, h):
        r0 = pl.multiple_of(g * SUBLANES, SUBLANES)
        ag = a_s[pl.ds(r0, SUBLANES), :]
        bg = b_s[pl.ds(r0, SUBLANES), :]
        for k in (1, 2, 4):
            a_sh = pltpu.roll(ag, k, 0)
            b_sh = pltpu.roll(bg, k, 0)
            keep = sub >= k
            bg = jnp.where(keep, bg + ag * b_sh, bg)
            ag = jnp.where(keep, ag * a_sh, ag)
        hs = bg + ag * h
        hs_s[pl.ds(r0, SUBLANES), :] = hs
        return hs[SUBLANES - 1:SUBLANES, :]

    h_last = lax.fori_loop(0, tm // SUBLANES, group, h_s[...])
    h_s[...] = h_last
    hlast_ref[...] = jnp.broadcast_to(h_last, (SUBLANES, D_RNN))
    rnn_ref[...] = (hs_s[...] * _gelu_tanh(yg)).astype(BF16)


def _front_call(x, g1, w_in, ropes, cw, cb, wg, bga, bgx, lam):
    s = x.shape[0]
    tm = TM_FRONT
    nt = s // tm
    n_last = MAX_WINDOW // tm
    row = lambda i: (i, 0)
    const2 = lambda i: (0, 0)
    const3 = lambda i: (0, 0, 0)
    last = lambda i: (jnp.maximum(i - (nt - n_last), 0), 0)
    t4 = 4 * BAND // tm
    t16 = ATT_BLOCK // tm
    out_shape = (
        jax.ShapeDtypeStruct((s, 3 * D_ATT), BF16),
        jax.ShapeDtypeStruct((s // 512, 4, BAND, 3 * D_ATT), BF16),
        jax.ShapeDtypeStruct((s // ATT_BLOCK, 16, BAND, 3 * D_ATT), BF16),
        jax.ShapeDtypeStruct((s, D_RNN), BF16),
        jax.ShapeDtypeStruct((MAX_WINDOW, D_ATT), F32),
        jax.ShapeDtypeStruct((MAX_WINDOW, D_ATT), F32),
        jax.ShapeDtypeStruct((SUBLANES, D_RNN), F32),
        jax.ShapeDtypeStruct((SUBLANES, D_RNN), F32),
    )
    out_specs = (
        pl.BlockSpec((tm, 3 * D_ATT), row),
        pl.BlockSpec((1, 4, tm // 4, 3 * D_ATT), lambda i: (i // t4, 0, i % t4, 0)),
        pl.BlockSpec((1, 16, tm // 16, 3 * D_ATT), lambda i: (i // t16, 0, i % t16, 0)),
        pl.BlockSpec((tm, D_RNN), row),
        pl.BlockSpec((tm, D_ATT), last),
        pl.BlockSpec((tm, D_ATT), last),
        pl.BlockSpec((SUBLANES, D_RNN), const2),
        pl.BlockSpec((SUBLANES, D_RNN), const2),
    )
    in_specs = [
        pl.BlockSpec((tm, D_MODEL), row),
        pl.BlockSpec((1, D_MODEL), const2),
        pl.BlockSpec((D_MODEL, D_IN), const2),
        pl.BlockSpec((tm, LANES), row),
        pl.BlockSpec((tm, LANES), row),
        pl.BlockSpec((tm, LANES), row),
        pl.BlockSpec((CONV_WIDTH, D_RNN), const2),
        pl.BlockSpec((1, D_RNN), const2),
        pl.BlockSpec((2, D_RNN // 2, D_RNN), const3),
        pl.BlockSpec((1, D_RNN), const2),
        pl.BlockSpec((1, D_RNN), const2),
        pl.BlockSpec((1, D_RNN), const2),
    ]
    scratch = [
        pltpu.VMEM((3 * N_SLABS, tm, LANES), F32),
        pltpu.VMEM((3 * N_SLABS, tm, LANES), F32),
        pltpu.VMEM((tm + SUBLANES, D_RNN), F32),
        pltpu.VMEM((tm, D_RNN), F32),
        pltpu.VMEM((tm, D_RNN), F32),
        pltpu.VMEM((tm, D_RNN), F32),
        pltpu.VMEM((1, D_RNN), F32),
    ]
    return pl.pallas_call(
        functools.partial(_front_kernel, tm=tm),
        out_shape=out_shape,
        grid=(nt,),
        in_specs=in_specs,
        out_specs=out_specs,
        scratch_shapes=scratch,
        compiler_params=pltpu.CompilerParams(
            dimension_semantics=("arbitrary",), vmem_limit_bytes=VMEM_LIMIT),
        name="prompt_front",
    )(x, g1, w_in, *ropes, cw, cb, wg, bga, bgx, lam)


def _band_bias():
    a = np.arange(BAND)[:, None]
    c = np.arange(2 * BAND)[None, :]
    dist = BAND + a - c
    ok = (dist >= 0) & (dist <= BAND)
    with_prev = np.where(ok, 0.0, NEG_BIG)
    no_prev = np.where(ok & (c >= BAND), 0.0, NEG_BIG)
    return np.stack([with_prev, no_prev]).astype(np.float32)


def _attn_kernel(bias_ref, *refs):
    pat_refs = [refs[5 * g:5 * g + 5] for g in range(3)]
    o_ref = refs[15]
    acc_s, m_s, l_s = refs[16:19]
    blk = pl.program_id(0)
    s = pl.program_id(1)

    @pl.when(s == 0)
    def _():
        acc_s[...] = jnp.zeros_like(acc_s)
        l_s[...] = jnp.zeros_like(l_s)
        m_s[...] = jnp.full(m_s.shape, NEG_BIG, F32)

    lane = lax.broadcasted_iota(jnp.int32, (BAND, LANES), 1)
    low = lane < HEAD_DIM

    first = (blk * 16 + s == 0, blk * 4 + s // 4 == 0, blk == 0)
    rows = (pl.ds(pl.multiple_of(s * BAND, BAND), BAND),
            pl.ds((s // 4) * (4 * BAND) + s % 4, BAND, stride=4),
            pl.ds(s, BAND, stride=16))

    for g in range(3):
        q_ref, kp_ref, kc_ref, vp_ref, vc_ref = pat_refs[g]
        bias = bias_ref[jnp.where(first[g], 1, 0)]
        for hp in range(N_SLABS):
            q2 = q_ref[:, hp * LANES:(hp + 1) * LANES]
            k2 = jnp.concatenate([kp_ref[:, hp * LANES:(hp + 1) * LANES],
                                  kc_ref[:, hp * LANES:(hp + 1) * LANES]], axis=0)
            v2 = jnp.concatenate([vp_ref[:, hp * LANES:(hp + 1) * LANES],
                                  vc_ref[:, hp * LANES:(hp + 1) * LANES]], axis=0)
            zero = jnp.zeros_like(q2)
            outs = []
            for sel in (low, jnp.logical_not(low)):
                sc = _dot_nt(jnp.where(sel, q2, zero), k2) + bias
                mx = jnp.max(sc, axis=1, keepdims=True)
                p = jnp.exp(sc - mx)
                outs.append((_dot(p.astype(BF16), v2), mx, jnp.sum(p, axis=1, keepdims=True)))
            o_new = jnp.where(low, outs[0][0], outs[1][0])
            m_new = jnp.where(low, outs[0][1], outs[1][1])
            l_new = jnp.where(low, outs[0][2], outs[1][2])
            m_old = m_s[hp, rows[g], :]
            m_tot = jnp.maximum(m_old, m_new)
            w_old = jnp.exp(m_old - m_tot)
            w_new = jnp.exp(m_new - m_tot)
            m_s[hp, rows[g], :] = m_tot
            l_s[hp, rows[g], :] = l_s[hp, rows[g], :] * w_old + l_new * w_new
            acc_s[hp, rows[g], :] = acc_s[hp, rows[g], :] * w_old + o_new * w_new

    @pl.when(s == pl.num_programs(1) - 1)
    def _():
        for hp in range(N_SLABS):
            o_ref[:, hp * LANES:(hp + 1) * LANES] = (acc_s[hp] / l_s[hp]).astype(BF16)


def _attn_call(qkv1, qkv4, qkv16):
    s = qkv1.shape[0]
    nb = s // ATT_BLOCK
    q1 = qkv1.reshape(s // BAND, BAND, 3 * D_ATT)
    blk = (None, BAND, D_ATT)
    blk4 = (None, None, BAND, D_ATT)

    def spec1(col, prev):
        def im(b, t):
            j = b * 16 + t
            return (jnp.maximum(j - 1, 0) if prev else j, 0, col)
        return pl.BlockSpec(blk, im)

    def spec4(col, prev):
        def im(b, t):
            j = b * 4 + t // 4
            return (jnp.maximum(j - 1, 0) if prev else j, t % 4, 0, col)
        return pl.BlockSpec(blk4, im)

    def spec16(col, prev):
        def im(b, t):
            return (jnp.maximum(b - 1, 0) if prev else b, t, 0, col)
        return pl.BlockSpec(blk4, im)

    in_specs = [pl.BlockSpec((2, BAND, 2 * BAND), lambda b, t: (0, 0, 0))]
    args = [jnp.asarray(_band_bias())]
    for arr, mk in ((q1, spec1), (qkv4, spec4), (qkv16, spec16)):
        in_specs += [mk(0, False), mk(1, True), mk(1, False), mk(2, True), mk(2, False)]
        args += [arr] * 5
    state = pltpu.VMEM((N_SLABS, ATT_BLOCK, LANES), F32)
    return pl.pallas_call(
        _attn_kernel,
        out_shape=jax.ShapeDtypeStruct((s, D_ATT), BF16),
        grid=(nb, 16),
        in_specs=in_specs,
        out_specs=pl.BlockSpec((ATT_BLOCK, D_ATT), lambda b, t: (b, 0)),
        scratch_shapes=[state, state, state],
        compiler_params=pltpu.CompilerParams(
            dimension_semantics=("arbitrary", "arbitrary"), vmem_limit_bytes=VMEM_LIMIT),
        name="prompt_attention",
    )(*args)


def _ffn_kernel(x_ref, att_ref, rnn_ref, wo_ref, g2_ref, wg_ref, wu_ref, wd_ref, gf_ref,
                o_ref, act_s, *, final_norm):
    y = (x_ref[...] + _dot(att_ref[...], wo_ref[0:D_ATT, :])
         + _dot(rnn_ref[...], wo_ref[D_ATT:D_MODEL, :]))
    hf = _rmsnorm(y, g2_ref[...]).astype(BF16)
    for c in range(D_FF // FF_CHUNK):
        cols = slice(c * FF_CHUNK, (c + 1) * FF_CHUNK)
        gate = _dot(hf, wg_ref[:, cols])
        up = _dot(hf, wu_ref[:, cols])
        act_s[:, cols] = (gate * jax.nn.sigmoid(gate) * up).astype(BF16)
    out = y + _dot(act_s[...], wd_ref[...])
    if final_norm:
        out = _rmsnorm(out, gf_ref[...])
    o_ref[...] = out


def _ffn_call(x, att, rnn, wo, g2, wg, wu, wd, gf, *, final_norm, tm):
    s = x.shape[0]
    row = lambda i: (i, 0)
    const = lambda i: (0, 0)
    once = pl.Buffered(1)
    in_specs = [
        pl.BlockSpec((tm, D_MODEL), row),
        pl.BlockSpec((tm, D_ATT), row),
        pl.BlockSpec((tm, D_RNN), row),
        pl.BlockSpec((D_MODEL, D_MODEL), const, pipeline_mode=once),
        pl.BlockSpec((1, D_MODEL), const),
        pl.BlockSpec((D_MODEL, D_FF), const, pipeline_mode=once),
        pl.BlockSpec((D_MODEL, D_FF), const, pipeline_mode=once),
        pl.BlockSpec((D_FF, D_MODEL), const, pipeline_mode=once),
        pl.BlockSpec((1, D_MODEL), const),
    ]
    return pl.pallas_call(
        functools.partial(_ffn_kernel, final_norm=final_norm),
        out_shape=jax.ShapeDtypeStruct((s, D_MODEL), F32),
        grid=(s // tm,),
        in_specs=in_specs,
        out_specs=pl.BlockSpec((tm, D_MODEL), row),
        scratch_shapes=[pltpu.VMEM((tm, D_FF), BF16)],
        compiler_params=pltpu.CompilerParams(
            dimension_semantics=("arbitrary",), vmem_limit_bytes=VMEM_LIMIT),
        name="out_proj_ffn",
    )(x, att, rnn, wo, g2, wg, wu, wd, gf)


def _sample_front_kernel(x_ref, g1_ref, w_ref, cos_ref, s1_ref, s2_ref, cw_ref, cb_ref, wg_ref,
                         bga_ref, bgx_ref, lam_ref, cst_ref, h0_ref,
                         q_ref, k_ref, v_ref, rnn_ref, nconv_ref, hl_ref, *, nb, nt):
    hn = _rmsnorm(x_ref[...], g1_ref[...]).astype(BF16)
    cos = cos_ref[...]
    s1 = s1_ref[...]
    s2 = s2_ref[...]
    for part, out in enumerate((q_ref, k_ref, v_ref)):
        z = _dot(hn, w_ref[:, part * D_ATT:(part + 1) * D_ATT])
        for c in range(N_SLABS):
            zc = z[:, c * LANES:(c + 1) * LANES]
            if part < 2:
                zc = _rope_slab(zc, cos, s1, s2)
            if part == 0:
                zc = zc * (1.0 / math.sqrt(HEAD_DIM))
            out[:, c * LANES:(c + 1) * LANES] = zc
    xr = _dot(hn, w_ref[:, 3 * D_ATT:3 * D_ATT + D_RNN])
    yg = _dot(hn, w_ref[:, 3 * D_ATT + D_RNN:])
    xpad = [cst_ref[j] for j in range(CONV_WIDTH - 1)]
    xpad += [xr[t * nb:(t + 1) * nb, :] for t in range(nt)]
    xc = []
    for t in range(nt):
        acc = cb_ref[...] + xpad[t] * cw_ref[0:1, :]
        for j in range(1, CONV_WIDTH):
            acc = acc + xpad[t + j] * cw_ref[j:j + 1, :]
        xc.append(acc)
    for j in range(CONV_WIDTH - 1):
        nconv_ref[j] = xpad[nt + j]
    a, b = _lru_coeffs(jnp.concatenate(xc, axis=0), wg_ref, bga_ref[...], bgx_ref[...],
                       lam_ref[...])
    h = h0_ref[...]
    hs = []
    for t in range(nt):
        h = a[t * nb:(t + 1) * nb, :] * h + b[t * nb:(t + 1) * nb, :]
        hs.append(h)
    hl_ref[...] = h
    rnn_ref[...] = (jnp.concatenate(hs, axis=0) * _gelu_tanh(yg)).astype(BF16)


def _sample_front_call(x, g1, w_in, ropes, cw, cb, wg, bga, bgx, lam, cst, h0, *, nb, nt):
    m = x.shape[0]
    out_shape = (
        jax.ShapeDtypeStruct((m, D_ATT), F32),
        jax.ShapeDtypeStruct((m, D_ATT), F32),
        jax.ShapeDtypeStruct((m, D_ATT), F32),
        jax.ShapeDtypeStruct((m, D_RNN), BF16),
        jax.ShapeDtypeStruct((CONV_WIDTH - 1, nb, D_RNN), F32),
        jax.ShapeDtypeStruct((nb, D_RNN), F32),
    )
    return pl.pallas_call(
        functools.partial(_sample_front_kernel, nb=nb, nt=nt),
        out_shape=out_shape,
        compiler_params=pltpu.CompilerParams(vmem_limit_bytes=VMEM_LIMIT),
        name="sample_front",
    )(x, g1, w_in, *ropes, cw, cb, wg, bga, bgx, lam, cst, h0)


def _sample_mult(nt, w_buf):
    i = np.arange(nt)[:, None]
    cache = np.zeros((SUBLANES, w_buf), np.float32)
    new = np.zeros((SUBLANES, LANES), np.float32)
    for dil in DILATIONS:
        for m in range(BAND + 1):
            idx = w_buf + np.arange(nt) - dil * m
            for t in range(nt):
                if idx[t] < 0:
                    continue
                if idx[t] < w_buf:
                    cache[t, idx[t]] += 1.0
                else:
                    new[t, LANES - nt + (idx[t] - w_buf)] += 1.0
    del i
    cache[nt:] = 1.0
    return cache, new


def _sample_attn_kernel(q_ref, kc_ref, vc_ref, kn_ref, vn_ref, mc_ref, mn_ref, *rest, nt, aliased):
    if aliased:
        rest = rest[2:]
    ko_ref, vo_ref, att_ref = rest
    w_buf = kc_ref.shape[-1]
    mc = mc_ref[...]
    mn = mn_ref[...]
    lane = lax.broadcasted_iota(jnp.int32, (HEAD_DIM, LANES), 1)
    fresh = lane >= LANES - nt
    for h in range(CACHE_HEADS):
        q = q_ref[0, h].astype(BF16)
        kc = kc_ref[0, 0, h]
        vc = vc_ref[0, 0, h]
        kn = kn_ref[0, h]
        vn = vn_ref[0, h]
        sc = jnp.where(mc > 0, _dot(q, kc.astype(BF16)), NEG_BIG)
        sn = jnp.where(mn > 0, _dot(q, kn.astype(BF16)), NEG_BIG)
        mx = jnp.maximum(jnp.max(sc, axis=1, keepdims=True), jnp.max(sn, axis=1, keepdims=True))
        pc = mc * jnp.exp(sc - mx)
        pn = mn * jnp.exp(sn - mx)
        den = jnp.sum(pc, axis=1, keepdims=True) + jnp.sum(pn, axis=1, keepdims=True)
        o = _dot_nt(pc.astype(BF16), vc.astype(BF16)) + _dot_nt(pn.astype(BF16), vn.astype(BF16))
        att_ref[0, h] = o / den
        for src, new, dst in ((kc, kn, ko_ref), (vc, vn, vo_ref)):
            moved = pltpu.roll(src, w_buf - nt, 1)
            dst[0, 0, h, :, 0:w_buf - LANES] = moved[:, 0:w_buf - LANES]
            dst[0, 0, h, :, w_buf - LANES:] = jnp.where(fresh, new, moved[:, w_buf - LANES:])


def _sample_attn_call(q, kc, vc, kn, vn, k_all, v_all, *, layer, depth, nt):
    _, nb, nh, _, w_buf = kc.shape
    hb = CACHE_HEADS
    mc, mn = _sample_mult(nt, w_buf)
    blk_c = pl.BlockSpec((1, 1, hb, HEAD_DIM, w_buf), lambda b, h: (layer, b, h, 0, 0))
    blk_n = pl.BlockSpec((1, hb, HEAD_DIM, LANES), lambda b, h: (b, h, 0, 0))
    blk_q = pl.BlockSpec((1, hb, SUBLANES, HEAD_DIM), lambda b, h: (b, h, 0, 0))
    blk_o = pl.BlockSpec((1, 1, hb, HEAD_DIM, w_buf), lambda b, h: (layer, b, h, 0, 0))
    const = lambda b, h: (0, 0)
    in_specs = [blk_q, blk_c, blk_c, blk_n, blk_n,
                pl.BlockSpec((SUBLANES, w_buf), const), pl.BlockSpec((SUBLANES, LANES), const)]
    args = [q, kc, vc, kn, vn, jnp.asarray(mc), jnp.asarray(mn)]
    aliases = {}
    aliased = k_all is not None
    if aliased:
        in_specs += [pl.BlockSpec(memory_space=pl.ANY), pl.BlockSpec(memory_space=pl.ANY)]
        args += [k_all, v_all]
        aliases = {7: 0, 8: 1}
    full = jax.ShapeDtypeStruct((depth, nb, nh, HEAD_DIM, w_buf), F32)
    return pl.pallas_call(
        functools.partial(_sample_attn_kernel, nt=nt, aliased=aliased),
        out_shape=(full, full, jax.ShapeDtypeStruct((nb, nh, SUBLANES, HEAD_DIM), F32)),
        grid=(nb, nh // hb),
        in_specs=in_specs,
        out_specs=(blk_o, blk_o, blk_q),
        input_output_aliases=aliases,
        compiler_params=pltpu.CompilerParams(
            dimension_semantics=("arbitrary", "arbitrary"), vmem_limit_bytes=VMEM_LIMIT),
        name="sample_attention",
    )(*args)


def _rope_tables(pos):
    half = ROT_DIM // 2
    inv = ROPE_THETA ** (-jnp.arange(half, dtype=F32) * 2.0 / ROT_DIM)
    ang = pos.astype(F32)[:, None] * inv[None, :]
    cos, sin = jnp.cos(ang), jnp.sin(ang)
    n = pos.shape[0]
    ones = jnp.ones((n, HEAD_DIM - ROT_DIM), F32)
    zeros = jnp.zeros((n, HEAD_DIM - ROT_DIM), F32)
    zh = jnp.zeros((n, half), F32)
    c = jnp.concatenate([cos, cos, ones], axis=1)
    s1 = jnp.concatenate([zh, sin, zeros], axis=1)
    s2 = jnp.concatenate([-sin, zh, zeros], axis=1)
    rep = LANES // HEAD_DIM
    return tuple(jnp.tile(t, (1, rep)) for t in (c, s1, s2))


def _gate_weights(w_a, w_x):
    def dense(w):
        return jax.scipy.linalg.block_diag(*[w[n] for n in range(N_RNN_BLOCKS)])
    da, dx = dense(w_a), dense(w_x)
    half = D_RNN // 2
    chunks = [jnp.concatenate([da[c * half:(c + 1) * half, c * half:(c + 1) * half],
                               dx[c * half:(c + 1) * half, c * half:(c + 1) * half]], axis=1)
              for c in range(2)]
    return jnp.stack(chunks).astype(BF16)


def kernel(x_prompt, x_sample, cache_k, cache_v, state_conv, state_h, norm1_g, w_in, conv_w, conv_b,
           w_gate_a, b_gate_a, w_gate_x, b_gate_x, lru_lambda, w_out, norm2_g, w_ffn_gate, w_ffn_up,
           w_ffn_down, final_norm_g):
    bp, s, _ = x_prompt.shape
    nb, nt, _ = x_sample.shape
    depth = norm1_g.shape[0]
    w_buf = cache_k.shape[2]
    assert bp == 1 and s % ATT_BLOCK == 0 and w_buf == MAX_WINDOW and nt * nb == BAND

    xp = x_prompt.reshape(s, D_MODEL)
    xs = x_sample.transpose(1, 0, 2).reshape(nt * nb, D_MODEL)
    rope_p = _rope_tables(jnp.arange(s, dtype=jnp.int32))
    rope_s = _rope_tables(PAST_LEN + jnp.repeat(jnp.arange(nt, dtype=jnp.int32), nb))
    ck_t = cache_k.transpose(0, 1, 3, 4, 2)
    cv_t = cache_v.transpose(0, 1, 3, 4, 2)
    cst = state_conv.transpose(0, 2, 1, 3)
    gf = final_norm_g.reshape(1, D_MODEL)
    row = lambda v: v.reshape(1, -1)

    pk, pv, pc, ph, sc, sh = [], [], [], [], [], []
    k_all = v_all = None
    for l in range(depth):
        w_in_b = w_in[l].astype(BF16)
        wo_b = w_out[l].astype(BF16)
        wg_b = w_ffn_gate[l].astype(BF16)
        wu_b = w_ffn_up[l].astype(BF16)
        wd_b = w_ffn_down[l].astype(BF16)
        gates = _gate_weights(w_gate_a[l], w_gate_x[l])
        common = (conv_w[l], row(conv_b[l]), gates, row(b_gate_a[l]), row(b_gate_x[l]),
                  row(lru_lambda[l]))
        final = l == depth - 1

        qkv1, qkv4, qkv16, rnn, k_last, v_last, tail, h_last = _front_call(
            xp, row(norm1_g[l]), w_in_b, rope_p, *common)
        att = _attn_call(qkv1, qkv4, qkv16)
        xp = _ffn_call(xp, att, rnn, wo_b, row(norm2_g[l]), wg_b, wu_b, wd_b, gf,
                       final_norm=final, tm=TM_FFN)
        pk.append(k_last.reshape(1, MAX_WINDOW, N_HEADS, HEAD_DIM))
        pv.append(v_last.reshape(1, MAX_WINDOW, N_HEADS, HEAD_DIM))
        pc.append(tail[SUBLANES - (CONV_WIDTH - 1):].reshape(1, CONV_WIDTH - 1, D_RNN))
        ph.append(h_last[0:1])

        q_s, k_s, v_s, rnn_s, nconv, hl = _sample_front_call(
            xs, row(norm1_g[l]), w_in_b, rope_s, *common, cst[l], state_h[l], nb=nb, nt=nt)

        def heads(v):
            return v.reshape(nt, nb, N_HEADS, HEAD_DIM).transpose(1, 2, 0, 3)

        q_b = jnp.pad(heads(q_s), ((0, 0), (0, 0), (0, SUBLANES - nt), (0, 0)))
        def fresh(v):
            return jnp.pad(heads(v).transpose(0, 1, 3, 2), ((0, 0), (0, 0), (0, 0), (LANES - nt, 0)))
        k_all, v_all, att_s = _sample_attn_call(
            q_b, ck_t, cv_t, fresh(k_s), fresh(v_s), k_all, v_all,
            layer=l, depth=depth, nt=nt)
        att_s = att_s[:, :, :nt].transpose(2, 0, 1, 3).reshape(nt * nb, D_ATT).astype(BF16)
        xs = _ffn_call(xs, att_s, rnn_s, wo_b, row(norm2_g[l]), wg_b, wu_b, wd_b, gf,
                       final_norm=final, tm=nt * nb)
        sc.append(nconv.transpose(1, 0, 2))
        sh.append(hl)

    y_prompt = xp.reshape(1, s, D_MODEL)
    y_sample = xs.reshape(nt, nb, D_MODEL).transpose(1, 0, 2)
    sample_k = k_all.transpose(0, 1, 4, 2, 3)
    sample_v = v_all.transpose(0, 1, 4, 2, 3)
    return (y_prompt, y_sample, jnp.stack(pk), jnp.stack(pv), jnp.stack(pc), jnp.stack(ph),
            sample_k, sample_v, jnp.stack(sc), jnp.stack(sh))
```

```python
import functools
import math

import numpy as np
import jax
import jax.numpy as jnp
from jax import lax
from jax.experimental import pallas as pl
from jax.experimental.pallas import tpu as pltpu

F32 = jnp.float32
BF16 = jnp.bfloat16

D_MODEL = 1024
HEAD_DIM = 64
D_ATT = 512
N_HEADS = 8
D_RNN = 512
N_RNN_BLOCKS = 8
CONV_WIDTH = 4
LRU_C = 8.0
DILATIONS = (1, 4, 16)
BAND = 128
MAX_WINDOW = 2048
ROPE_THETA = 500000.0
ROT_DIM = 16
D_FF = 2816
D_IN = 3 * D_ATT + 2 * D_RNN
RMS_EPS = 1e-6
PAST_LEN = 16384

LANES = 128
SUBLANES = 8
N_SLABS = D_ATT // LANES
FF_CHUNK = 256
NEG_BIG = -1e30
VMEM_LIMIT = 56 * 1024 * 1024

TM_FRONT = 512
TM_FFN = 512
ATT_BLOCK = MAX_WINDOW
ATT_STEPS = ATT_BLOCK // BAND
CACHE_HEADS = 4
QKV_SLOT = (2, 0, 1)


def _dot(a, b):
    return jnp.dot(a, b, preferred_element_type=F32)


def _dot_nt(a, b):
    return lax.dot_general(a, b, (((1,), (1,)), ((), ())), preferred_element_type=F32)


def _rmsnorm(x, g):
    return x * lax.rsqrt(jnp.mean(x * x, axis=-1, keepdims=True) + RMS_EPS) * g


def _softplus(x):
    return jnp.maximum(x, 0.0) + jnp.log1p(jnp.exp(-jnp.abs(x)))


def _gelu_tanh(x):
    return 0.5 * x * (1.0 + jnp.tanh(math.sqrt(2.0 / math.pi) * (x + 0.044715 * (x * x * x))))


def _rope_slab(x, c, s1, s2):
    return x * c + pltpu.roll(x, 8, 1) * s1 + pltpu.roll(x, LANES - 8, 1) * s2


def _lru_coeffs(xc, wg_ref, bga, bgx, lam):
    xb = xc.astype(BF16)
    half = D_RNN // 2
    g0 = _dot(xb[:, :half], wg_ref[0])
    g1 = _dot(xb[:, half:], wg_ref[1])
    r = jax.nn.sigmoid(jnp.concatenate([g0[:, :half], g1[:, :half]], axis=1) + bga)
    ig = jax.nn.sigmoid(jnp.concatenate([g0[:, half:], g1[:, half:]], axis=1) + bgx)
    log_a = (-LRU_C) * r * _softplus(-lam)
    a = jnp.exp(log_a)
    th = jnp.tanh(log_a)
    b = jnp.sqrt(-2.0 * th / (1.0 - th)) * (ig * xc)
    return a, b


def _shift_rows(x, k, fill):
    row = lax.broadcasted_iota(jnp.int32, x.shape, 0)
    return jnp.where(row >= k, pltpu.roll(x, k, 0), fill)


def _front_kernel(x_ref, g1_ref, w_ref, cos_ref, s1_ref, s2_ref, cw_ref, cb_ref, wg_ref,
                  bga_ref, bgx_ref, lam_ref,
                  qkv1_ref, qkv4_ref, qkv16_ref, rnn_ref, klast_ref, vlast_ref, tail_ref, hlast_ref,
                  xs_s, hn_s, cls_s, nat_s, rn_s, carry_s, h_s, *, tm):
    i = pl.program_id(0)
    grp = tm // 4
    n16 = tm // 16

    @pl.when(i == 0)
    def _():
        carry_s[...] = jnp.zeros_like(carry_s)
        h_s[...] = jnp.zeros_like(h_s)

    hn32 = _rmsnorm(x_ref[...], g1_ref[...])
    n_x = D_MODEL // LANES
    for c in range(n_x):
        xs_s[c] = hn32[:, c * LANES:(c + 1) * LANES]
    for j in range(4):
        for c in range(n_x):
            hn_s[j * grp:(j + 1) * grp, c * LANES:(c + 1) * LANES] = (
                xs_s[c, pl.ds(j, grp, stride=4), :].astype(BF16))

    xr = _dot(hn_s[...], w_ref[:, 3 * D_ATT:3 * D_ATT + D_RNN])
    yg = _dot(hn_s[...], w_ref[:, 3 * D_ATT + D_RNN:])
    slab = [xr[j * grp:(j + 1) * grp, :] for j in range(4)]
    back = [None] + [_shift_rows(slab[j], 1, carry_s[j - 1, SUBLANES - 1:SUBLANES, :])
                     for j in range(1, 4)]
    for j in range(1, 4):
        carry_s[j - 1] = slab[j][grp - SUBLANES:, :]
        tail_ref[j - 1] = slab[j][grp - SUBLANES:, :]
    w = [cw_ref[k:k + 1, :] for k in range(CONV_WIDTH)]
    taps = ((slab[0], back[3], back[2], back[1]),
            (slab[1], slab[0], back[3], back[2]),
            (slab[2], slab[1], slab[0], back[3]),
            (slab[3], slab[2], slab[1], slab[0]))
    xc = jnp.concatenate(
        [cb_ref[...] + t[0] * w[3] + t[1] * w[2] + t[2] * w[1] + t[3] * w[0] for t in taps], axis=0)
    a, b = _lru_coeffs(xc, wg_ref, bga_ref[...], bgx_ref[...], lam_ref[...])

    pp = [a[0:grp, :]]
    hh = [b[0:grp, :]]
    for j in range(1, 4):
        aj = a[j * grp:(j + 1) * grp, :]
        hh.append(aj * hh[j - 1] + b[j * grp:(j + 1) * grp, :])
        pp.append(aj * pp[j - 1])
    pc, hc = pp[3], hh[3]
    row = lax.broadcasted_iota(jnp.int32, (grp, D_RNN), 0)
    k = 1
    while k < grp:
        keep = row >= k
        hc = jnp.where(keep, hc + pc * pltpu.roll(hc, k, 0), hc)
        pc = jnp.where(keep, pc * pltpu.roll(pc, k, 0), pc)
        k *= 2
    h_in = h_s[...]
    ends = hc + pc * h_in
    h_prev = _shift_rows(ends, 1, h_in)
    h_s[...] = ends[grp - 1:grp, :]
    hlast_ref[...] = jnp.broadcast_to(ends[grp - 1:grp, :], (SUBLANES, D_RNN))
    for j in range(4):
        gated = (hh[j] + pp[j] * h_prev) * _gelu_tanh(yg[j * grp:(j + 1) * grp, :])
        for c in range(N_SLABS):
            rn_s[c, pl.ds(j, grp, stride=4), :] = gated[:, c * LANES:(c + 1) * LANES]
    for c in range(N_SLABS):
        rnn_ref[:, c * LANES:(c + 1) * LANES] = rn_s[c].astype(BF16)

    cos = cos_ref[...]
    s1 = s1_ref[...]
    s2 = s2_ref[...]
    for part in range(3):
        z = _dot(hn_s[...], w_ref[:, part * D_ATT:(part + 1) * D_ATT])
        for c in range(N_SLABS):
            zc = z[:, c * LANES:(c + 1) * LANES]
            if part < 2:
                zc = _rope_slab(zc, cos, s1, s2)
            if part == 0:
                zc = zc * (1.0 / math.sqrt(HEAD_DIM))
            n = QKV_SLOT[part] * N_SLABS + c
            cols = slice(n * LANES, (n + 1) * LANES)
            cls_s[n] = zc
            for j in range(4):
                rows = zc[j * grp:(j + 1) * grp, :]
                qkv4_ref[0, j, :, cols] = rows.astype(BF16)
                nat_s[n, pl.ds(j, grp, stride=4), :] = rows
            for j in range(4):
                for jj in range(4):
                    rows = cls_s[n, pl.ds(j * grp + jj, n16, stride=4), :]
                    qkv16_ref[0, j + 4 * jj, :, cols] = rows.astype(BF16)
            nat = nat_s[n]
            qkv1_ref[:, cols] = nat.astype(BF16)
            if part == 1:
                klast_ref[:, c * LANES:(c + 1) * LANES] = nat
            if part == 2:
                vlast_ref[:, c * LANES:(c + 1) * LANES] = nat


def _class_major_positions(s, tm):
    r = np.arange(s)
    tile, within = r // tm, r % tm
    grp = tm // 4
    return (tile * tm + 4 * (within % grp) + within // grp).astype(np.int32)


def _front_call(x, g1, w_in, ropes, cw, cb, wg, bga, bgx, lam):
    s = x.shape[0]
    tm = TM_FRONT
    nt = s // tm
    n_last = MAX_WINDOW // tm
    row = lambda i: (i, 0)
    const2 = lambda i: (0, 0)
    const3 = lambda i: (0, 0, 0)
    last = lambda i: (jnp.maximum(i - (nt - n_last), 0), 0)
    t4 = 4 * BAND // tm
    t16 = ATT_BLOCK // tm
    out_shape = (
        jax.ShapeDtypeStruct((s, 3 * D_ATT), BF16),
        jax.ShapeDtypeStruct((s // (4 * BAND), 4, BAND, 3 * D_ATT), BF16),
        jax.ShapeDtypeStruct((s // ATT_BLOCK, ATT_STEPS, BAND, 3 * D_ATT), BF16),
        jax.ShapeDtypeStruct((s, D_RNN), BF16),
        jax.ShapeDtypeStruct((MAX_WINDOW, D_ATT), F32),
        jax.ShapeDtypeStruct((MAX_WINDOW, D_ATT), F32),
        jax.ShapeDtypeStruct((CONV_WIDTH - 1, SUBLANES, D_RNN), F32),
        jax.ShapeDtypeStruct((SUBLANES, D_RNN), F32),
    )
    out_specs = (
        pl.BlockSpec((tm, 3 * D_ATT), row),
        pl.BlockSpec((1, 4, tm // 4, 3 * D_ATT), lambda i: (i // t4, 0, i % t4, 0)),
        pl.BlockSpec((1, ATT_STEPS, tm // ATT_STEPS, 3 * D_ATT), lambda i: (i // t16, 0, i % t16, 0)),
        pl.BlockSpec((tm, D_RNN), row),
        pl.BlockSpec((tm, D_ATT), last),
        pl.BlockSpec((tm, D_ATT), last),
        pl.BlockSpec((CONV_WIDTH - 1, SUBLANES, D_RNN), const3),
        pl.BlockSpec((SUBLANES, D_RNN), const2),
    )
    in_specs = [
        pl.BlockSpec((tm, D_MODEL), row),
        pl.BlockSpec((1, D_MODEL), const2),
        pl.BlockSpec((D_MODEL, D_IN), const2, pipeline_mode=pl.Buffered(1)),
        pl.BlockSpec((tm, LANES), row),
        pl.BlockSpec((tm, LANES), row),
        pl.BlockSpec((tm, LANES), row),
        pl.BlockSpec((CONV_WIDTH, D_RNN), const2),
        pl.BlockSpec((1, D_RNN), const2),
        pl.BlockSpec((2, D_RNN // 2, D_RNN), const3),
        pl.BlockSpec((1, D_RNN), const2),
        pl.BlockSpec((1, D_RNN), const2),
        pl.BlockSpec((1, D_RNN), const2),
    ]
    scratch = [
        pltpu.VMEM((D_MODEL // LANES, tm, LANES), F32),
        pltpu.VMEM((tm, D_MODEL), BF16),
        pltpu.VMEM((3 * N_SLABS, tm, LANES), F32),
        pltpu.VMEM((3 * N_SLABS, tm, LANES), F32),
        pltpu.VMEM((N_SLABS, tm, LANES), F32),
        pltpu.VMEM((CONV_WIDTH - 1, SUBLANES, D_RNN), F32),
        pltpu.VMEM((1, D_RNN), F32),
    ]
    return pl.pallas_call(
        functools.partial(_front_kernel, tm=tm),
        out_shape=out_shape,
        grid=(nt,),
        in_specs=in_specs,
        out_specs=out_specs,
        scratch_shapes=scratch,
        compiler_params=pltpu.CompilerParams(
            dimension_semantics=("arbitrary",), vmem_limit_bytes=VMEM_LIMIT),
        name="prompt_front",
    )(x, g1, w_in, *ropes, cw, cb, wg, bga, bgx, lam)


def _band_bias():
    a = np.arange(BAND)[:, None]
    c = np.arange(2 * BAND)[None, :]
    dist = BAND + a - c
    ok = (dist >= 0) & (dist <= BAND)
    with_prev = np.where(ok, 0.0, NEG_BIG)
    no_prev = np.where(ok & (c >= BAND), 0.0, NEG_BIG)
    return np.tile(np.stack([with_prev, no_prev]), (1, 2, 1)).astype(np.float32)


def _attn_kernel(bias_ref, *refs):
    pat_refs = [refs[3 * g:3 * g + 3] for g in range(3)]
    o_ref = refs[9]
    acc_s, m_s, l_s = refs[10:13]
    blk = pl.program_id(0)
    s = pl.program_id(1)

    @pl.when(s == 0)
    def _():
        acc_s[...] = jnp.zeros_like(acc_s)
        l_s[...] = jnp.zeros_like(l_s)
        m_s[...] = jnp.full(m_s.shape, NEG_BIG, F32)

    lane = lax.broadcasted_iota(jnp.int32, (BAND, LANES), 1)
    low = lane < HEAD_DIM

    first = (blk * ATT_STEPS + s == 0, blk * 4 + s // 4 == 0, blk == 0)
    rows = (pl.ds(pl.multiple_of(s * BAND, BAND), BAND),
            pl.ds((s // 4) * (4 * BAND) + s % 4, BAND, stride=4),
            pl.ds(s, BAND, stride=ATT_STEPS))

    for g in range(3):
        q_ref, prev_ref, cur_ref = pat_refs[g]
        bias = bias_ref[jnp.where(first[g], 1, 0)]
        for hp in range(N_SLABS):
            kcols = slice(hp * LANES, (hp + 1) * LANES)
            vcols = slice(D_ATT + hp * LANES, D_ATT + (hp + 1) * LANES)
            q2 = q_ref[:, kcols]
            k2 = jnp.concatenate([prev_ref[:, kcols], cur_ref[:, kcols]], axis=0)
            v2 = jnp.concatenate([prev_ref[:, vcols], cur_ref[:, vcols]], axis=0)
            zero = jnp.zeros_like(q2)
            qq = jnp.concatenate([jnp.where(low, q2, zero), jnp.where(low, zero, q2)], axis=0)
            sc = _dot_nt(qq, k2) + bias
            mx = jnp.max(sc, axis=1, keepdims=True)
            p = jnp.exp(sc - mx)
            sm = jnp.sum(p, axis=1, keepdims=True)
            oo = _dot(p.astype(BF16), v2)
            o_new = jnp.where(low, oo[:BAND], oo[BAND:])
            m_new = jnp.where(low, mx[:BAND], mx[BAND:])
            l_new = jnp.where(low, sm[:BAND], sm[BAND:])
            m_old = m_s[hp, rows[g], :]
            m_tot = jnp.maximum(m_old, m_new)
            w_old = jnp.exp(m_old - m_tot)
            w_new = jnp.exp(m_new - m_tot)
            m_s[hp, rows[g], :] = m_tot
            l_s[hp, rows[g], :] = l_s[hp, rows[g], :] * w_old + l_new * w_new
            acc_s[hp, rows[g], :] = acc_s[hp, rows[g], :] * w_old + o_new * w_new

    @pl.when(s == pl.num_programs(1) - 1)
    def _():
        for hp in range(N_SLABS):
            o_ref[:, hp * LANES:(hp + 1) * LANES] = (acc_s[hp] / l_s[hp]).astype(BF16)


def _attn_call(qkv1, qkv4, qkv16):
    s = qkv1.shape[0]
    nb = s // ATT_BLOCK
    q1 = qkv1.reshape(s // BAND, BAND, 3 * D_ATT)

    def width(is_q):
        return D_ATT if is_q else 2 * D_ATT

    def col(is_q):
        return QKV_SLOT[0] if is_q else 0

    def spec1(is_q, prev):
        def im(b, t):
            j = b * ATT_STEPS + t
            return (jnp.maximum(j - 1, 0) if prev else j, 0, col(is_q))
        return pl.BlockSpec((None, BAND, width(is_q)), im)

    def spec4(is_q, prev):
        def im(b, t):
            j = b * 4 + t // 4
            return (jnp.maximum(j - 1, 0) if prev else j, t % 4, 0, col(is_q))
        return pl.BlockSpec((None, None, BAND, width(is_q)), im)

    def spec16(is_q, prev):
        def im(b, t):
            return (jnp.maximum(b - 1, 0) if prev else b, t, 0, col(is_q))
        return pl.BlockSpec((None, None, BAND, width(is_q)), im)

    in_specs = [pl.BlockSpec((2, 2 * BAND, 2 * BAND), lambda b, t: (0, 0, 0))]
    args = [jnp.asarray(_band_bias())]
    for arr, mk in ((q1, spec1), (qkv4, spec4), (qkv16, spec16)):
        in_specs += [mk(True, False), mk(False, True), mk(False, False)]
        args += [arr] * 3
    state = pltpu.VMEM((N_SLABS, ATT_BLOCK, LANES), F32)
    return pl.pallas_call(
        _attn_kernel,
        out_shape=jax.ShapeDtypeStruct((s, D_ATT), BF16),
        grid=(nb, ATT_STEPS),
        in_specs=in_specs,
        out_specs=pl.BlockSpec((ATT_BLOCK, D_ATT), lambda b, t: (b, 0)),
        scratch_shapes=[state, state, state],
        compiler_params=pltpu.CompilerParams(
            dimension_semantics=("arbitrary", "arbitrary"), vmem_limit_bytes=VMEM_LIMIT),
        name="prompt_attention",
    )(*args)


def _ffn_kernel(x_ref, att_ref, rnn_ref, wo_ref, g2_ref, wg_ref, wu_ref, wd_ref, gf_ref,
                o_ref, act_s, *, final_norm):
    y = (x_ref[...] + _dot(att_ref[...], wo_ref[0:D_ATT, :])
         + _dot(rnn_ref[...], wo_ref[D_ATT:D_MODEL, :]))
    hf = _rmsnorm(y, g2_ref[...]).astype(BF16)
    for c in range(D_FF // FF_CHUNK):
        cols = slice(c * FF_CHUNK, (c + 1) * FF_CHUNK)
        gate = _dot(hf, wg_ref[:, cols])
        up = _dot(hf, wu_ref[:, cols])
        act_s[:, cols] = (gate * jax.nn.sigmoid(gate) * up).astype(BF16)
    out = y + _dot(act_s[...], wd_ref[...])
    if final_norm:
        out = _rmsnorm(out, gf_ref[...])
    o_ref[...] = out


def _ffn_call(x, att, rnn, wo, g2, wg, wu, wd, gf, *, final_norm, tm):
    s = x.shape[0]
    row = lambda i: (i, 0)
    const = lambda i: (0, 0)
    once = pl.Buffered(1)
    in_specs = [
        pl.BlockSpec((tm, D_MODEL), row),
        pl.BlockSpec((tm, D_ATT), row),
        pl.BlockSpec((tm, D_RNN), row),
        pl.BlockSpec((D_MODEL, D_MODEL), const, pipeline_mode=once),
        pl.BlockSpec((1, D_MODEL), const),
        pl.BlockSpec((D_MODEL, D_FF), const, pipeline_mode=once),
        pl.BlockSpec((D_MODEL, D_FF), const, pipeline_mode=once),
        pl.BlockSpec((D_FF, D_MODEL), const, pipeline_mode=once),
        pl.BlockSpec((1, D_MODEL), const),
    ]
    return pl.pallas_call(
        functools.partial(_ffn_kernel, final_norm=final_norm),
        out_shape=jax.ShapeDtypeStruct((s, D_MODEL), F32),
        grid=(s // tm,),
        in_specs=in_specs,
        out_specs=pl.BlockSpec((tm, D_MODEL), row),
        scratch_shapes=[pltpu.VMEM((tm, D_FF), BF16)],
        compiler_params=pltpu.CompilerParams(
            dimension_semantics=("arbitrary",), vmem_limit_bytes=VMEM_LIMIT),
        name="out_proj_ffn",
    )(x, att, rnn, wo, g2, wg, wu, wd, gf)


def _sample_front_kernel(x_ref, g1_ref, w_ref, cos_ref, s1_ref, s2_ref, cw_ref, cb_ref, wg_ref,
                         bga_ref, bgx_ref, lam_ref, cst_ref, h0_ref,
                         q_ref, k_ref, v_ref, rnn_ref, nconv_ref, hl_ref, *, nb, nt):
    hn = _rmsnorm(x_ref[...], g1_ref[...]).astype(BF16)
    cos = cos_ref[...]
    s1 = s1_ref[...]
    s2 = s2_ref[...]
    for part, out in enumerate((q_ref, k_ref, v_ref)):
        z = _dot(hn, w_ref[:, part * D_ATT:(part + 1) * D_ATT])
        for c in range(N_SLABS):
            zc = z[:, c * LANES:(c + 1) * LANES]
            if part < 2:
                zc = _rope_slab(zc, cos, s1, s2)
            if part == 0:
                zc = zc * (1.0 / math.sqrt(HEAD_DIM))
            out[:, c * LANES:(c + 1) * LANES] = zc
    xr = _dot(hn, w_ref[:, 3 * D_ATT:3 * D_ATT + D_RNN])
    yg = _dot(hn, w_ref[:, 3 * D_ATT + D_RNN:])
    xpad = [cst_ref[j] for j in range(CONV_WIDTH - 1)]
    xpad += [xr[t * nb:(t + 1) * nb, :] for t in range(nt)]
    xc = []
    for t in range(nt):
        acc = cb_ref[...] + xpad[t] * cw_ref[0:1, :]
        for j in range(1, CONV_WIDTH):
            acc = acc + xpad[t + j] * cw_ref[j:j + 1, :]
        xc.append(acc)
    for j in range(CONV_WIDTH - 1):
        nconv_ref[j] = xpad[nt + j]
    a, b = _lru_coeffs(jnp.concatenate(xc, axis=0), wg_ref, bga_ref[...], bgx_ref[...],
                       lam_ref[...])
    h = h0_ref[...]
    hs = []
    for t in range(nt):
        h = a[t * nb:(t + 1) * nb, :] * h + b[t * nb:(t + 1) * nb, :]
        hs.append(h)
    hl_ref[...] = h
    rnn_ref[...] = (jnp.concatenate(hs, axis=0) * _gelu_tanh(yg)).astype(BF16)


def _sample_front_call(x, g1, w_in, ropes, cw, cb, wg, bga, bgx, lam, cst, h0, *, nb, nt):
    m = x.shape[0]
    out_shape = (
        jax.ShapeDtypeStruct((m, D_ATT), F32),
        jax.ShapeDtypeStruct((m, D_ATT), F32),
        jax.ShapeDtypeStruct((m, D_ATT), F32),
        jax.ShapeDtypeStruct((m, D_RNN), BF16),
        jax.ShapeDtypeStruct((CONV_WIDTH - 1, nb, D_RNN), F32),
        jax.ShapeDtypeStruct((nb, D_RNN), F32),
    )
    return pl.pallas_call(
        functools.partial(_sample_front_kernel, nb=nb, nt=nt),
        out_shape=out_shape,
        compiler_params=pltpu.CompilerParams(vmem_limit_bytes=VMEM_LIMIT),
        name="sample_front",
    )(x, g1, w_in, *ropes, cw, cb, wg, bga, bgx, lam, cst, h0)


def _sample_mult(nt, w_buf):
    cache = np.zeros((SUBLANES, w_buf), np.float32)
    new = np.zeros((SUBLANES, LANES), np.float32)
    for dil in DILATIONS:
        for m in range(BAND + 1):
            idx = w_buf + np.arange(nt) - dil * m
            for t in range(nt):
                if idx[t] < 0:
                    continue
                if idx[t] < w_buf:
                    cache[t, idx[t]] += 1.0
                else:
                    new[t, LANES - nt + (idx[t] - w_buf)] += 1.0
    cache[nt:] = 1.0
    return cache, new


def _sample_attn_kernel(q_ref, kc_ref, vc_ref, kn_ref, vn_ref, mc_ref, mn_ref, *rest, nt, aliased):
    if aliased:
        rest = rest[2:]
    ko_ref, vo_ref, att_ref = rest
    w_buf = kc_ref.shape[-1]
    mc = mc_ref[...]
    mn = mn_ref[...]
    lane = lax.broadcasted_iota(jnp.int32, (HEAD_DIM, LANES), 1)
    fresh = lane >= LANES - nt
    for h in range(CACHE_HEADS):
        q = q_ref[0, h].astype(BF16)
        kc = kc_ref[0, 0, h]
        vc = vc_ref[0, 0, h]
        kn = kn_ref[0, h]
        vn = vn_ref[0, h]
        sc = jnp.where(mc > 0, _dot(q, kc.astype(BF16)), NEG_BIG)
        sn = jnp.where(mn > 0, _dot(q, kn.astype(BF16)), NEG_BIG)
        mx = jnp.maximum(jnp.max(sc, axis=1, keepdims=True), jnp.max(sn, axis=1, keepdims=True))
        pc = mc * jnp.exp(sc - mx)
        pn = mn * jnp.exp(sn - mx)
        den = jnp.sum(pc, axis=1, keepdims=True) + jnp.sum(pn, axis=1, keepdims=True)
        o = _dot_nt(pc.astype(BF16), vc.astype(BF16)) + _dot_nt(pn.astype(BF16), vn.astype(BF16))
        att_ref[0, h] = o / den
        for src, new, dst in ((kc, kn, ko_ref), (vc, vn, vo_ref)):
            moved = pltpu.roll(src, w_buf - nt, 1)
            dst[0, 0, h, :, 0:w_buf - LANES] = moved[:, 0:w_buf - LANES]
            dst[0, 0, h, :, w_buf - LANES:] = jnp.where(fresh, new, moved[:, w_buf - LANES:])


def _sample_attn_call(q, kc, vc, kn, vn, k_all, v_all, *, layer, depth, nt):
    _, nb, nh, _, w_buf = kc.shape
    hb = CACHE_HEADS
    mc, mn = _sample_mult(nt, w_buf)
    blk_c = pl.BlockSpec((1, 1, hb, HEAD_DIM, w_buf), lambda b, h: (layer, b, h, 0, 0))
    blk_n = pl.BlockSpec((1, hb, HEAD_DIM, LANES), lambda b, h: (b, h, 0, 0))
    blk_q = pl.BlockSpec((1, hb, SUBLANES, HEAD_DIM), lambda b, h: (b, h, 0, 0))
    blk_o = pl.BlockSpec((1, 1, hb, HEAD_DIM, w_buf), lambda b, h: (layer, b, h, 0, 0))
    const = lambda b, h: (0, 0)
    in_specs = [blk_q, blk_c, blk_c, blk_n, blk_n,
                pl.BlockSpec((SUBLANES, w_buf), const), pl.BlockSpec((SUBLANES, LANES), const)]
    args = [q, kc, vc, kn, vn, jnp.asarray(mc), jnp.asarray(mn)]
    aliases = {}
    aliased = k_all is not None
    if aliased:
        in_specs += [pl.BlockSpec(memory_space=pl.ANY), pl.BlockSpec(memory_space=pl.ANY)]
        args += [k_all, v_all]
        aliases = {7: 0, 8: 1}
    full = jax.ShapeDtypeStruct((depth, nb, nh, HEAD_DIM, w_buf), F32)
    return pl.pallas_call(
        functools.partial(_sample_attn_kernel, nt=nt, aliased=aliased),
        out_shape=(full, full, jax.ShapeDtypeStruct((nb, nh, SUBLANES, HEAD_DIM), F32)),
        grid=(nb, nh // hb),
        in_specs=in_specs,
        out_specs=(blk_o, blk_o, blk_q),
        input_output_aliases=aliases,
        compiler_params=pltpu.CompilerParams(
            dimension_semantics=("arbitrary", "arbitrary"), vmem_limit_bytes=VMEM_LIMIT),
        name="sample_attention",
    )(*args)


def _rope_tables(pos):
    half = ROT_DIM // 2
    inv = ROPE_THETA ** (-jnp.arange(half, dtype=F32) * 2.0 / ROT_DIM)
    ang = pos.astype(F32)[:, None] * inv[None, :]
    cos, sin = jnp.cos(ang), jnp.sin(ang)
    n = pos.shape[0]
    ones = jnp.ones((n, HEAD_DIM - ROT_DIM), F32)
    zeros = jnp.zeros((n, HEAD_DIM - ROT_DIM), F32)
    zh = jnp.zeros((n, half), F32)
    c = jnp.concatenate([cos, cos, ones], axis=1)
    s1 = jnp.concatenate([zh, sin, zeros], axis=1)
    s2 = jnp.concatenate([-sin, zh, zeros], axis=1)
    rep = LANES // HEAD_DIM
    return tuple(jnp.tile(t, (1, rep)) for t in (c, s1, s2))


def _gate_weights(w_a, w_x):
    def dense(w):
        return jax.scipy.linalg.block_diag(*[w[n] for n in range(N_RNN_BLOCKS)])
    da, dx = dense(w_a), dense(w_x)
    half = D_RNN // 2
    chunks = [jnp.concatenate([da[c * half:(c + 1) * half, c * half:(c + 1) * half],
                               dx[c * half:(c + 1) * half, c * half:(c + 1) * half]], axis=1)
              for c in range(2)]
    return jnp.stack(chunks).astype(BF16)


def kernel(x_prompt, x_sample, cache_k, cache_v, state_conv, state_h, norm1_g, w_in, conv_w, conv_b,
           w_gate_a, b_gate_a, w_gate_x, b_gate_x, lru_lambda, w_out, norm2_g, w_ffn_gate, w_ffn_up,
           w_ffn_down, final_norm_g):
    bp, s, _ = x_prompt.shape
    nb, nt, _ = x_sample.shape
    depth = norm1_g.shape[0]
    w_buf = cache_k.shape[2]
    assert bp == 1 and s % ATT_BLOCK == 0 and w_buf == MAX_WINDOW and nt * nb == BAND

    xp = x_prompt.reshape(s, D_MODEL)
    xs = x_sample.transpose(1, 0, 2).reshape(nt * nb, D_MODEL)
    rope_p = _rope_tables(jnp.asarray(_class_major_positions(s, TM_FRONT)))
    rope_s = _rope_tables(PAST_LEN + jnp.repeat(jnp.arange(nt, dtype=jnp.int32), nb))
    ck_t = cache_k.transpose(0, 1, 3, 4, 2)
    cv_t = cache_v.transpose(0, 1, 3, 4, 2)
    cst = state_conv.transpose(0, 2, 1, 3)
    gf = final_norm_g.reshape(1, D_MODEL)
    row = lambda v: v.reshape(1, -1)

    pk, pv, pc, ph, sc, sh = [], [], [], [], [], []
    k_all = v_all = None
    for l in range(depth):
        w_in_b = w_in[l].astype(BF16)
        wo_b = w_out[l].astype(BF16)
        wg_b = w_ffn_gate[l].astype(BF16)
        wu_b = w_ffn_up[l].astype(BF16)
        wd_b = w_ffn_down[l].astype(BF16)
        gates = _gate_weights(w_gate_a[l], w_gate_x[l])
        common = (conv_w[l], row(conv_b[l]), gates, row(b_gate_a[l]), row(b_gate_x[l]),
                  row(lru_lambda[l]))
        final = l == depth - 1

        qkv1, qkv4, qkv16, rnn, k_last, v_last, tail, h_last = _front_call(
            xp, row(norm1_g[l]), w_in_b, rope_p, *common)
        att = _attn_call(qkv1, qkv4, qkv16)
        xp = _ffn_call(xp, att, rnn, wo_b, row(norm2_g[l]), wg_b, wu_b, wd_b, gf,
                       final_norm=final, tm=TM_FFN)
        pk.append(k_last.reshape(1, MAX_WINDOW, N_HEADS, HEAD_DIM))
        pv.append(v_last.reshape(1, MAX_WINDOW, N_HEADS, HEAD_DIM))
        pc.append(tail[:, SUBLANES - 1, :].reshape(1, CONV_WIDTH - 1, D_RNN))
        ph.append(h_last[0:1])

        q_s, k_s, v_s, rnn_s, nconv, hl = _sample_front_call(
            xs, row(norm1_g[l]), w_in_b, rope_s, *common, cst[l], state_h[l], nb=nb, nt=nt)

        def heads(v):
            return v.reshape(nt, nb, N_HEADS, HEAD_DIM).transpose(1, 2, 0, 3)

        q_b = jnp.pad(heads(q_s), ((0, 0), (0, 0), (0, SUBLANES - nt), (0, 0)))

        def fresh(v):
            return jnp.pad(heads(v).transpose(0, 1, 3, 2), ((0, 0), (0, 0), (0, 0), (LANES - nt, 0)))

        k_all, v_all, att_s = _sample_attn_call(
            q_b, ck_t, cv_t, fresh(k_s), fresh(v_s), k_all, v_all,
            layer=l, depth=depth, nt=nt)
        att_s = att_s[:, :, :nt].transpose(2, 0, 1, 3).reshape(nt * nb, D_ATT).astype(BF16)
        xs = _ffn_call(xs, att_s, rnn_s, wo_b, row(norm2_g[l]), wg_b, wu_b, wd_b, gf,
                       final_norm=final, tm=nt * nb)
        sc.append(nconv.transpose(1, 0, 2))
        sh.append(hl)

    y_prompt = xp.reshape(1, s, D_MODEL)
    y_sample = xs.reshape(nt, nb, D_MODEL).transpose(1, 0, 2)
    sample_k = k_all.transpose(0, 1, 4, 2, 3)
    sample_v = v_all.transpose(0, 1, 4, 2, 3)
    return (y_prompt, y_sample, jnp.stack(pk), jnp.stack(pv), jnp.stack(pc), jnp.stack(ph),
            sample_k, sample_v, jnp.stack(sc), jnp.stack(sh))
```

```python
import functools
import math

import numpy as np
import jax
import jax.numpy as jnp
from jax import lax
from jax.experimental import pallas as pl
from jax.experimental.pallas import tpu as pltpu

F32 = jnp.float32
BF16 = jnp.bfloat16

D_MODEL = 1024
HEAD_DIM = 64
D_ATT = 512
N_HEADS = 8
D_RNN = 512
N_RNN_BLOCKS = 8
CONV_WIDTH = 4
LRU_C = 8.0
DILATIONS = (1, 4, 16)
BAND = 128
MAX_WINDOW = 2048
ROPE_THETA = 500000.0
ROT_DIM = 16
D_FF = 2816
D_IN = 3 * D_ATT + 2 * D_RNN
RMS_EPS = 1e-6
PAST_LEN = 16384

LANES = 128
SUBLANES = 8
N_SLABS = D_ATT // LANES
FF_CHUNK = 256
NEG_BIG = -1e30
VMEM_LIMIT = 56 * 1024 * 1024

TM_FRONT = 512
TM_FFN = 512
ATT_BLOCK = MAX_WINDOW
ATT_STEPS = ATT_BLOCK // BAND
CACHE_HEADS = 4
QKV_SLOT = (2, 0, 1)


def _dot(a, b):
    return jnp.dot(a, b, preferred_element_type=F32)


def _dot_nt(a, b):
    return lax.dot_general(a, b, (((1,), (1,)), ((), ())), preferred_element_type=F32)


def _rmsnorm(x, g):
    return x * lax.rsqrt(jnp.mean(x * x, axis=-1, keepdims=True) + RMS_EPS) * g


def _softplus(x):
    return jnp.maximum(x, 0.0) + jnp.log1p(jnp.exp(-jnp.abs(x)))


def _gelu_tanh(x):
    return 0.5 * x * (1.0 + jnp.tanh(math.sqrt(2.0 / math.pi) * (x + 0.044715 * (x * x * x))))


def _rope_slab(x, c, t):
    half = ROT_DIM // 2
    lane = lax.broadcasted_iota(jnp.int32, x.shape, 1)
    partner = jnp.where((lane & half) == 0, pltpu.roll(x, LANES - half, 1), pltpu.roll(x, half, 1))
    return x * c + partner * t


def _lru_coeffs(xc, wg_ref, bga, bgx, lam):
    xb = xc.astype(BF16)
    half = D_RNN // 2
    g0 = _dot(xb[:, :half], wg_ref[0])
    g1 = _dot(xb[:, half:], wg_ref[1])
    r = jax.nn.sigmoid(jnp.concatenate([g0[:, :half], g1[:, :half]], axis=1) + bga)
    ig = jax.nn.sigmoid(jnp.concatenate([g0[:, half:], g1[:, half:]], axis=1) + bgx)
    log_a = (-LRU_C) * r * _softplus(-lam)
    a = jnp.exp(log_a)
    th = jnp.tanh(log_a)
    b = jnp.sqrt(-2.0 * th / (1.0 - th)) * (ig * xc)
    return a, b


def _shift_rows(x, k, fill):
    row = lax.broadcasted_iota(jnp.int32, x.shape, 0)
    return jnp.where(row >= k, pltpu.roll(x, k, 0), fill)


def _front_kernel(x_ref, g1_ref, w_ref, cos_ref, sin_ref, cw_ref, cb_ref, wg_ref,
                  bga_ref, bgx_ref, lam_ref,
                  qkv1_ref, qkv4_ref, qkv16_ref, rnn_ref, klast_ref, vlast_ref, tail_ref, hlast_ref,
                  xs_s, hn_s, cls_s, nat_s, rn_s, carry_s, h_s, *, tm):
    i = pl.program_id(0)
    grp = tm // 4
    n16 = tm // 16

    @pl.when(i == 0)
    def _():
        carry_s[...] = jnp.zeros_like(carry_s)
        h_s[...] = jnp.zeros_like(h_s)

    hn32 = _rmsnorm(x_ref[...], g1_ref[...])
    n_x = D_MODEL // LANES
    for c in range(n_x):
        xs_s[c] = hn32[:, c * LANES:(c + 1) * LANES]
    for j in range(4):
        for c in range(n_x):
            hn_s[j * grp:(j + 1) * grp, c * LANES:(c + 1) * LANES] = (
                xs_s[c, pl.ds(j, grp, stride=4), :].astype(BF16))

    xr = _dot(hn_s[...], w_ref[:, 3 * D_ATT:3 * D_ATT + D_RNN])
    yg = _dot(hn_s[...], w_ref[:, 3 * D_ATT + D_RNN:])
    slab = [xr[j * grp:(j + 1) * grp, :] for j in range(4)]
    back = [None] + [_shift_rows(slab[j], 1, carry_s[j - 1, SUBLANES - 1:SUBLANES, :])
                     for j in range(1, 4)]
    for j in range(1, 4):
        carry_s[j - 1] = slab[j][grp - SUBLANES:, :]
        tail_ref[j - 1] = slab[j][grp - SUBLANES:, :]
    w = [cw_ref[k:k + 1, :] for k in range(CONV_WIDTH)]
    taps = ((slab[0], back[3], back[2], back[1]),
            (slab[1], slab[0], back[3], back[2]),
            (slab[2], slab[1], slab[0], back[3]),
            (slab[3], slab[2], slab[1], slab[0]))
    xc = jnp.concatenate(
        [cb_ref[...] + t[0] * w[3] + t[1] * w[2] + t[2] * w[1] + t[3] * w[0] for t in taps], axis=0)
    a, b = _lru_coeffs(xc, wg_ref, bga_ref[...], bgx_ref[...], lam_ref[...])

    pp = [a[0:grp, :]]
    hh = [b[0:grp, :]]
    for j in range(1, 4):
        aj = a[j * grp:(j + 1) * grp, :]
        hh.append(aj * hh[j - 1] + b[j * grp:(j + 1) * grp, :])
        pp.append(aj * pp[j - 1])
    pc, hc = pp[3], hh[3]
    row = lax.broadcasted_iota(jnp.int32, (grp, D_RNN), 0)
    k = 1
    while k < grp:
        keep = row >= k
        hc = jnp.where(keep, hc + pc * pltpu.roll(hc, k, 0), hc)
        pc = jnp.where(keep, pc * pltpu.roll(pc, k, 0), pc)
        k *= 2
    h_in = h_s[...]
    ends = hc + pc * h_in
    h_prev = _shift_rows(ends, 1, h_in)
    h_s[...] = ends[grp - 1:grp, :]
    hlast_ref[...] = jnp.broadcast_to(ends[grp - 1:grp, :], (SUBLANES, D_RNN))
    for j in range(4):
        gated = (hh[j] + pp[j] * h_prev) * _gelu_tanh(yg[j * grp:(j + 1) * grp, :])
        for c in range(N_SLABS):
            rn_s[c, pl.ds(j, grp, stride=4), :] = gated[:, c * LANES:(c + 1) * LANES]
    for c in range(N_SLABS):
        rnn_ref[:, c * LANES:(c + 1) * LANES] = rn_s[c].astype(BF16)

    cos = cos_ref[...]
    sin = sin_ref[...]
    for part in range(3):
        z = _dot(hn_s[...], w_ref[:, part * D_ATT:(part + 1) * D_ATT])
        for c in range(N_SLABS):
            zc = z[:, c * LANES:(c + 1) * LANES]
            if part < 2:
                zc = _rope_slab(zc, cos, sin)
            if part == 0:
                zc = zc * (1.0 / math.sqrt(HEAD_DIM))
            n = QKV_SLOT[part] * N_SLABS + c
            cols = slice(n * LANES, (n + 1) * LANES)
            cls_s[n] = zc
            for j in range(4):
                rows = zc[j * grp:(j + 1) * grp, :]
                qkv4_ref[0, j, :, cols] = rows.astype(BF16)
                nat_s[n, pl.ds(j, grp, stride=4), :] = rows
            for j in range(4):
                for jj in range(4):
                    rows = cls_s[n, pl.ds(j * grp + jj, n16, stride=4), :]
                    qkv16_ref[0, j + 4 * jj, :, cols] = rows.astype(BF16)
            nat = nat_s[n]
            qkv1_ref[:, cols] = nat.astype(BF16)
            if part == 1:
                klast_ref[:, c * LANES:(c + 1) * LANES] = nat
            if part == 2:
                vlast_ref[:, c * LANES:(c + 1) * LANES] = nat


def _class_major_positions(s, tm):
    r = np.arange(s)
    tile, within = r // tm, r % tm
    grp = tm // 4
    return (tile * tm + 4 * (within % grp) + within // grp).astype(np.int32)


def _front_call(x, g1, w_in, rope, cw, cb, wg, bga, bgx, lam, *, layer):
    s = x.shape[0]
    tm = TM_FRONT
    nt = s // tm
    n_last = MAX_WINDOW // tm
    row = lambda i: (i, 0)
    const2 = lambda i: (0, 0)
    const3 = lambda i: (0, 0, 0)
    last = lambda i: (jnp.maximum(i - (nt - n_last), 0), 0)
    t4 = 4 * BAND // tm
    t16 = ATT_BLOCK // tm
    out_shape = (
        jax.ShapeDtypeStruct((s, 3 * D_ATT), BF16),
        jax.ShapeDtypeStruct((s // (4 * BAND), 4, BAND, 3 * D_ATT), BF16),
        jax.ShapeDtypeStruct((s // ATT_BLOCK, ATT_STEPS, BAND, 3 * D_ATT), BF16),
        jax.ShapeDtypeStruct((s, D_RNN), BF16),
        jax.ShapeDtypeStruct((MAX_WINDOW, D_ATT), F32),
        jax.ShapeDtypeStruct((MAX_WINDOW, D_ATT), F32),
        jax.ShapeDtypeStruct((CONV_WIDTH - 1, SUBLANES, D_RNN), F32),
        jax.ShapeDtypeStruct((SUBLANES, D_RNN), F32),
    )
    out_specs = (
        pl.BlockSpec((tm, 3 * D_ATT), row),
        pl.BlockSpec((1, 4, tm // 4, 3 * D_ATT), lambda i: (i // t4, 0, i % t4, 0)),
        pl.BlockSpec((1, ATT_STEPS, tm // ATT_STEPS, 3 * D_ATT), lambda i: (i // t16, 0, i % t16, 0)),
        pl.BlockSpec((tm, D_RNN), row),
        pl.BlockSpec((tm, D_ATT), last),
        pl.BlockSpec((tm, D_ATT), last),
        pl.BlockSpec((CONV_WIDTH - 1, SUBLANES, D_RNN), const3),
        pl.BlockSpec((SUBLANES, D_RNN), const2),
    )
    in_specs = [
        pl.BlockSpec((tm, D_MODEL), row),
        pl.BlockSpec((1, D_MODEL), const2),
        pl.BlockSpec((None, D_MODEL, D_IN), lambda i: (layer, 0, 0), pipeline_mode=pl.Buffered(1)),
        pl.BlockSpec((tm, LANES), lambda i: (i, 0)),
        pl.BlockSpec((tm, LANES), lambda i: (i, 1)),
        pl.BlockSpec((CONV_WIDTH, D_RNN), const2),
        pl.BlockSpec((1, D_RNN), const2),
        pl.BlockSpec((2, D_RNN // 2, D_RNN), const3),
        pl.BlockSpec((1, D_RNN), const2),
        pl.BlockSpec((1, D_RNN), const2),
        pl.BlockSpec((1, D_RNN), const2),
    ]
    scratch = [
        pltpu.VMEM((D_MODEL // LANES, tm, LANES), F32),
        pltpu.VMEM((tm, D_MODEL), BF16),
        pltpu.VMEM((3 * N_SLABS, tm, LANES), F32),
        pltpu.VMEM((3 * N_SLABS, tm, LANES), F32),
        pltpu.VMEM((N_SLABS, tm, LANES), F32),
        pltpu.VMEM((CONV_WIDTH - 1, SUBLANES, D_RNN), F32),
        pltpu.VMEM((1, D_RNN), F32),
    ]
    return pl.pallas_call(
        functools.partial(_front_kernel, tm=tm),
        out_shape=out_shape,
        grid=(nt,),
        in_specs=in_specs,
        out_specs=out_specs,
        scratch_shapes=scratch,
        compiler_params=pltpu.CompilerParams(
            dimension_semantics=("arbitrary",), vmem_limit_bytes=VMEM_LIMIT),
        name="prompt_front",
    )(x, g1, w_in, rope, rope, cw, cb, wg, bga, bgx, lam)


def _band_bias():
    a = np.arange(BAND)[:, None]
    c = np.arange(2 * BAND)[None, :]
    dist = BAND + a - c
    ok = (dist >= 0) & (dist <= BAND)
    with_prev = np.where(ok, 0.0, NEG_BIG)
    no_prev = np.where(ok & (c >= BAND), 0.0, NEG_BIG)
    return np.tile(np.stack([with_prev, no_prev]), (1, 2, 1)).astype(np.float32)


def _merge_rows(hp, rows, o_new, m_new, l_new, acc_s, m_s, l_s):
    m_old = m_s[hp, rows, :]
    m_tot = jnp.maximum(m_old, m_new)
    w_old = jnp.exp(m_old - m_tot)
    w_new = jnp.exp(m_new - m_tot)
    m_s[hp, rows, :] = m_tot
    l_s[hp, rows, :] = l_s[hp, rows, :] * w_old + l_new * w_new
    acc_s[hp, rows, :] = acc_s[hp, rows, :] * w_old + o_new * w_new


def _attn_kernel(bias_ref, *refs):
    pat_refs = [refs[3 * g:3 * g + 3] for g in range(3)]
    o_ref = refs[9]
    acc_s, m_s, l_s, tmp_s, nat_s = refs[10:15]
    blk = pl.program_id(0)
    s = pl.program_id(1)
    quarter = ATT_BLOCK // 4
    piece = BAND // 4

    @pl.when(s == 0)
    def _():
        acc_s[...] = jnp.zeros_like(acc_s)
        l_s[...] = jnp.zeros_like(l_s)
        m_s[...] = jnp.full(m_s.shape, NEG_BIG, F32)

    lane = lax.broadcasted_iota(jnp.int32, (BAND, LANES), 1)
    low = lane < HEAD_DIM

    first = (blk * ATT_STEPS + s == 0, blk * 4 + s // 4 == 0, blk == 0)
    base4 = (s % 4) * quarter
    rows = (None,
            pl.ds(pl.multiple_of(base4 + (s // 4) * BAND, BAND), BAND),
            pl.ds(base4 + s // 4, BAND, stride=4))

    for g in range(3):
        q_ref, prev_ref, cur_ref = pat_refs[g]
        bias = bias_ref[jnp.where(first[g], 1, 0)]
        for hp in range(N_SLABS):
            kcols = slice(hp * LANES, (hp + 1) * LANES)
            vcols = slice(D_ATT + hp * LANES, D_ATT + (hp + 1) * LANES)
            q2 = q_ref[:, kcols]
            k2 = jnp.concatenate([prev_ref[:, kcols], cur_ref[:, kcols]], axis=0)
            v2 = jnp.concatenate([prev_ref[:, vcols], cur_ref[:, vcols]], axis=0)
            zero = jnp.zeros_like(q2)
            qq = jnp.concatenate([jnp.where(low, q2, zero), jnp.where(low, zero, q2)], axis=0)
            sc = _dot_nt(qq, k2) + bias
            mx = jnp.max(sc, axis=1, keepdims=True)
            p = jnp.exp(sc - mx)
            sm = jnp.sum(p, axis=1, keepdims=True)
            oo = _dot(p.astype(BF16), v2)
            o_new = jnp.where(low, oo[:BAND], oo[BAND:])
            m_new = jnp.where(low, mx[:BAND], mx[BAND:])
            l_new = jnp.where(low, sm[:BAND], sm[BAND:])
            if g == 0:
                for n, val in enumerate((o_new, m_new, l_new)):
                    tmp_s[3 * hp + n] = val
                for c4 in range(4):
                    pick = pl.ds(c4, piece, stride=4)
                    dst = pl.ds(pl.multiple_of(c4 * quarter + s * piece, piece), piece)
                    _merge_rows(hp, dst, tmp_s[3 * hp, pick, :], tmp_s[3 * hp + 1, pick, :],
                                tmp_s[3 * hp + 2, pick, :], acc_s, m_s, l_s)
            else:
                _merge_rows(hp, rows[g], o_new, m_new, l_new, acc_s, m_s, l_s)

    @pl.when(s == pl.num_programs(1) - 1)
    def _():
        for hp in range(N_SLABS):
            for c4 in range(4):
                part = slice(c4 * quarter, (c4 + 1) * quarter)
                nat_s[pl.ds(c4, quarter, stride=4), :] = acc_s[hp, part, :] / l_s[hp, part, :]
            o_ref[:, hp * LANES:(hp + 1) * LANES] = nat_s[...].astype(BF16)


def _attn_call(qkv1, qkv4, qkv16):
    s = qkv1.shape[0]
    nb = s // ATT_BLOCK
    q1 = qkv1.reshape(s // BAND, BAND, 3 * D_ATT)

    def width(is_q):
        return D_ATT if is_q else 2 * D_ATT

    def col(is_q):
        return QKV_SLOT[0] if is_q else 0

    def spec1(is_q, prev):
        def im(b, t):
            j = b * ATT_STEPS + t
            return (jnp.maximum(j - 1, 0) if prev else j, 0, col(is_q))
        return pl.BlockSpec((None, BAND, width(is_q)), im)

    def spec4(is_q, prev):
        def im(b, t):
            j = b * 4 + t // 4
            return (jnp.maximum(j - 1, 0) if prev else j, t % 4, 0, col(is_q))
        return pl.BlockSpec((None, None, BAND, width(is_q)), im)

    def spec16(is_q, prev):
        def im(b, t):
            return (jnp.maximum(b - 1, 0) if prev else b, t, 0, col(is_q))
        return pl.BlockSpec((None, None, BAND, width(is_q)), im)

    in_specs = [pl.BlockSpec((2, 2 * BAND, 2 * BAND), lambda b, t: (0, 0, 0))]
    args = [jnp.asarray(_band_bias())]
    for arr, mk in ((q1, spec1), (qkv4, spec4), (qkv16, spec16)):
        in_specs += [mk(True, False), mk(False, True), mk(False, False)]
        args += [arr] * 3
    state = pltpu.VMEM((N_SLABS, ATT_BLOCK, LANES), F32)
    return pl.pallas_call(
        _attn_kernel,
        out_shape=jax.ShapeDtypeStruct((s, D_ATT), BF16),
        grid=(nb, ATT_STEPS),
        in_specs=in_specs,
        out_specs=pl.BlockSpec((ATT_BLOCK, D_ATT), lambda b, t: (b, 0)),
        scratch_shapes=[state, state, state,
                        pltpu.VMEM((3 * N_SLABS, BAND, LANES), F32),
                        pltpu.VMEM((ATT_BLOCK, LANES), F32)],
        compiler_params=pltpu.CompilerParams(
            dimension_semantics=("arbitrary", "arbitrary"), vmem_limit_bytes=VMEM_LIMIT),
        name="prompt_attention",
    )(*args)


def _ffn_kernel(x_ref, att_ref, rnn_ref, wo_ref, g2_ref, wg_ref, wu_ref, wd_ref, gf_ref,
                o_ref, act_s, *, final_norm):
    y = (x_ref[...] + _dot(att_ref[...], wo_ref[0:D_ATT, :])
         + _dot(rnn_ref[...], wo_ref[D_ATT:D_MODEL, :]))
    hf = _rmsnorm(y, g2_ref[...]).astype(BF16)
    for c in range(D_FF // FF_CHUNK):
        cols = slice(c * FF_CHUNK, (c + 1) * FF_CHUNK)
        gate = _dot(hf, wg_ref[:, cols])
        up = _dot(hf, wu_ref[:, cols])
        act_s[:, cols] = (gate * jax.nn.sigmoid(gate) * up).astype(BF16)
    out = y + _dot(act_s[...], wd_ref[...])
    if final_norm:
        out = _rmsnorm(out, gf_ref[...])
    o_ref[...] = out


def _ffn_call(x, att, rnn, wo, g2, wg, wu, wd, gf, *, layer, final_norm, tm):
    s = x.shape[0]
    row = lambda i: (i, 0)
    const = lambda i: (0, 0)
    pick = lambda i: (layer, 0, 0)
    once = pl.Buffered(1)
    in_specs = [
        pl.BlockSpec((tm, D_MODEL), row),
        pl.BlockSpec((tm, D_ATT), row),
        pl.BlockSpec((tm, D_RNN), row),
        pl.BlockSpec((None, D_MODEL, D_MODEL), pick, pipeline_mode=once),
        pl.BlockSpec((1, D_MODEL), const),
        pl.BlockSpec((None, D_MODEL, D_FF), pick, pipeline_mode=once),
        pl.BlockSpec((None, D_MODEL, D_FF), pick, pipeline_mode=once),
        pl.BlockSpec((None, D_FF, D_MODEL), pick, pipeline_mode=once),
        pl.BlockSpec((1, D_MODEL), const),
    ]
    return pl.pallas_call(
        functools.partial(_ffn_kernel, final_norm=final_norm),
        out_shape=jax.ShapeDtypeStruct((s, D_MODEL), F32),
        grid=(s // tm,),
        in_specs=in_specs,
        out_specs=pl.BlockSpec((tm, D_MODEL), row),
        scratch_shapes=[pltpu.VMEM((tm, D_FF), BF16)],
        compiler_params=pltpu.CompilerParams(
            dimension_semantics=("arbitrary",), vmem_limit_bytes=VMEM_LIMIT),
        name="out_proj_ffn",
    )(x, att, rnn, wo, g2, wg, wu, wd, gf)


def _sample_front_kernel(x_ref, g1_ref, w_ref, cos_ref, sin_ref, cw_ref, cb_ref, wg_ref,
                         bga_ref, bgx_ref, lam_ref, cst_ref, h0_ref,
                         q_ref, k_ref, v_ref, rnn_ref, nconv_ref, hl_ref, *, nb, nt):
    hn = _rmsnorm(x_ref[...], g1_ref[...]).astype(BF16)
    cos = cos_ref[...]
    sin = sin_ref[...]
    for part, out in enumerate((q_ref, k_ref, v_ref)):
        z = _dot(hn, w_ref[:, part * D_ATT:(part + 1) * D_ATT])
        for c in range(N_SLABS):
            zc = z[:, c * LANES:(c + 1) * LANES]
            if part < 2:
                zc = _rope_slab(zc, cos, sin)
            if part == 0:
                zc = zc * (1.0 / math.sqrt(HEAD_DIM))
            out[:, c * LANES:(c + 1) * LANES] = zc
    xr = _dot(hn, w_ref[:, 3 * D_ATT:3 * D_ATT + D_RNN])
    yg = _dot(hn, w_ref[:, 3 * D_ATT + D_RNN:])
    xpad = [cst_ref[j] for j in range(CONV_WIDTH - 1)]
    xpad += [xr[t * nb:(t + 1) * nb, :] for t in range(nt)]
    xc = []
    for t in range(nt):
        acc = cb_ref[...] + xpad[t] * cw_ref[0:1, :]
        for j in range(1, CONV_WIDTH):
            acc = acc + xpad[t + j] * cw_ref[j:j + 1, :]
        xc.append(acc)
    for j in range(CONV_WIDTH - 1):
        nconv_ref[j] = xpad[nt + j]
    a, b = _lru_coeffs(jnp.concatenate(xc, axis=0), wg_ref, bga_ref[...], bgx_ref[...],
                       lam_ref[...])
    h = h0_ref[...]
    hs = []
    for t in range(nt):
        h = a[t * nb:(t + 1) * nb, :] * h + b[t * nb:(t + 1) * nb, :]
        hs.append(h)
    hl_ref[...] = h
    rnn_ref[...] = (jnp.concatenate(hs, axis=0) * _gelu_tanh(yg)).astype(BF16)


def _sample_front_call(x, g1, w_in, rope, cw, cb, wg, bga, bgx, lam, cst, h0, *, layer, nb, nt):
    m = x.shape[0]

    def whole(a):
        return pl.BlockSpec(a.shape, lambda i, nd=a.ndim: (0,) * nd)

    in_specs = [whole(x), whole(g1),
                pl.BlockSpec((None, D_MODEL, D_IN), lambda i: (layer, 0, 0)),
                pl.BlockSpec((m, LANES), lambda i: (0, 0)), pl.BlockSpec((m, LANES), lambda i: (0, 1))]
    in_specs += [whole(a) for a in (cw, cb, wg, bga, bgx, lam, cst, h0)]
    out_shape = (
        jax.ShapeDtypeStruct((m, D_ATT), F32),
        jax.ShapeDtypeStruct((m, D_ATT), F32),
        jax.ShapeDtypeStruct((m, D_ATT), F32),
        jax.ShapeDtypeStruct((m, D_RNN), BF16),
        jax.ShapeDtypeStruct((CONV_WIDTH - 1, nb, D_RNN), F32),
        jax.ShapeDtypeStruct((nb, D_RNN), F32),
    )
    out_specs = tuple(pl.BlockSpec(o.shape, lambda i, nd=len(o.shape): (0,) * nd) for o in out_shape)
    return pl.pallas_call(
        functools.partial(_sample_front_kernel, nb=nb, nt=nt),
        out_shape=out_shape,
        grid=(1,),
        in_specs=in_specs,
        out_specs=out_specs,
        compiler_params=pltpu.CompilerParams(
            dimension_semantics=("arbitrary",), vmem_limit_bytes=VMEM_LIMIT),
        name="sample_front",
    )(x, g1, w_in, rope, rope, cw, cb, wg, bga, bgx, lam, cst, h0)


def _sample_mult(nt, w_buf):
    cache = np.zeros((SUBLANES, w_buf), np.float32)
    new = np.zeros((SUBLANES, LANES), np.float32)
    for dil in DILATIONS:
        for m in range(BAND + 1):
            idx = w_buf + np.arange(nt) - dil * m
            for t in range(nt):
                if idx[t] < 0:
                    continue
                if idx[t] < w_buf:
                    cache[t, idx[t]] += 1.0
                else:
                    new[t, LANES - nt + (idx[t] - w_buf)] += 1.0
    cache[nt:] = 1.0
    return cache, new


def _sample_attn_kernel(q_ref, kc_ref, vc_ref, kn_ref, vn_ref, mc_ref, mn_ref, *rest, nt, aliased):
    if aliased:
        rest = rest[2:]
    ko_ref, vo_ref, att_ref = rest
    w_buf = kc_ref.shape[-1]
    mc = mc_ref[...]
    mn = mn_ref[...]
    lane = lax.broadcasted_iota(jnp.int32, (HEAD_DIM, LANES), 1)
    fresh = lane >= LANES - nt

    def tail_tiles(new_ref, pair):
        rows = jnp.concatenate([new_ref[0, :, pair * LANES:(pair + 1) * LANES],
                                jnp.zeros((LANES - SUBLANES, LANES), F32)], axis=0)
        placed = pltpu.roll(rows.T, LANES - nt, 1)
        return placed[:HEAD_DIM], placed[HEAD_DIM:]

    kn_tiles, vn_tiles = [], []
    for pair in range(CACHE_HEADS // 2):
        kn_tiles += tail_tiles(kn_ref, pair)
        vn_tiles += tail_tiles(vn_ref, pair)
    for h in range(CACHE_HEADS):
        q = q_ref[0, h].astype(BF16)
        kc = kc_ref[0, 0, h]
        vc = vc_ref[0, 0, h]
        kn = kn_tiles[h]
        vn = vn_tiles[h]
        sc = jnp.where(mc > 0, _dot(q, kc.astype(BF16)), NEG_BIG)
        sn = jnp.where(mn > 0, _dot(q, kn.astype(BF16)), NEG_BIG)
        mx = jnp.maximum(jnp.max(sc, axis=1, keepdims=True), jnp.max(sn, axis=1, keepdims=True))
        pc = mc * jnp.exp(sc - mx)
        pn = mn * jnp.exp(sn - mx)
        den = jnp.sum(pc, axis=1, keepdims=True) + jnp.sum(pn, axis=1, keepdims=True)
        o = _dot_nt(pc.astype(BF16), vc.astype(BF16)) + _dot_nt(pn.astype(BF16), vn.astype(BF16))
        att_ref[0, h] = o / den
        for src, new, dst in ((kc, kn, ko_ref), (vc, vn, vo_ref)):
            moved = pltpu.roll(src, w_buf - nt, 1)
            dst[0, 0, h, :, 0:w_buf - LANES] = moved[:, 0:w_buf - LANES]
            dst[0, 0, h, :, w_buf - LANES:] = jnp.where(fresh, new, moved[:, w_buf - LANES:])


def _sample_attn_call(q, kc, vc, kn, vn, k_all, v_all, *, layer, depth, nt):
    _, nb, nh, _, w_buf = kc.shape
    hb = CACHE_HEADS
    mc, mn = _sample_mult(nt, w_buf)
    blk_c = pl.BlockSpec((1, 1, hb, HEAD_DIM, w_buf), lambda b, h: (layer, b, h, 0, 0))
    blk_n = pl.BlockSpec((1, SUBLANES, hb * HEAD_DIM), lambda b, h: (b, 0, h))
    blk_q = pl.BlockSpec((1, hb, SUBLANES, HEAD_DIM), lambda b, h: (b, h, 0, 0))
    blk_o = pl.BlockSpec((1, 1, hb, HEAD_DIM, w_buf), lambda b, h: (layer, b, h, 0, 0))
    const = lambda b, h: (0, 0)
    in_specs = [blk_q, blk_c, blk_c, blk_n, blk_n,
                pl.BlockSpec((SUBLANES, w_buf), const), pl.BlockSpec((SUBLANES, LANES), const)]
    args = [q, kc, vc, kn, vn, jnp.asarray(mc), jnp.asarray(mn)]
    aliases = {}
    aliased = k_all is not None
    if aliased:
        in_specs += [pl.BlockSpec(memory_space=pl.ANY), pl.BlockSpec(memory_space=pl.ANY)]
        args += [k_all, v_all]
        aliases = {7: 0, 8: 1}
    full = jax.ShapeDtypeStruct((depth, nb, nh, HEAD_DIM, w_buf), F32)
    return pl.pallas_call(
        functools.partial(_sample_attn_kernel, nt=nt, aliased=aliased),
        out_shape=(full, full, jax.ShapeDtypeStruct((nb, nh, SUBLANES, HEAD_DIM), F32)),
        grid=(nb, nh // hb),
        in_specs=in_specs,
        out_specs=(blk_o, blk_o, blk_q),
        input_output_aliases=aliases,
        compiler_params=pltpu.CompilerParams(
            dimension_semantics=("arbitrary", "arbitrary"), vmem_limit_bytes=VMEM_LIMIT),
        name="sample_attention",
    )(*args)


def _rope_tables(pos):
    half = ROT_DIM // 2
    inv = ROPE_THETA ** (-jnp.arange(half, dtype=F32) * 2.0 / ROT_DIM)
    ang = pos.astype(F32)[:, None] * inv[None, :]
    cos, sin = jnp.cos(ang), jnp.sin(ang)
    packed = jnp.concatenate([cos, -sin, sin], axis=1)
    expand = np.zeros((3 * half, 2 * LANES), np.float32)
    base = np.zeros((1, 2 * LANES), np.float32)
    for lane in range(LANES):
        d = lane % HEAD_DIM
        if d < half:
            expand[d, lane] = 1.0
            expand[half + d, LANES + lane] = 1.0
        elif d < ROT_DIM:
            expand[d - half, lane] = 1.0
            expand[2 * half + d - half, LANES + lane] = 1.0
        else:
            base[0, lane] = 1.0
    return jnp.dot(packed, jnp.asarray(expand), precision=lax.Precision.HIGHEST) + jnp.asarray(base)


def _gate_weights(w_a, w_x):
    def dense(w):
        return jax.scipy.linalg.block_diag(*[w[n] for n in range(N_RNN_BLOCKS)])
    da, dx = dense(w_a), dense(w_x)
    half = D_RNN // 2
    chunks = [jnp.concatenate([da[c * half:(c + 1) * half, c * half:(c + 1) * half],
                               dx[c * half:(c + 1) * half, c * half:(c + 1) * half]], axis=1)
              for c in range(2)]
    return jnp.stack(chunks).astype(BF16)


def kernel(x_prompt, x_sample, cache_k, cache_v, state_conv, state_h, norm1_g, w_in, conv_w, conv_b,
           w_gate_a, b_gate_a, w_gate_x, b_gate_x, lru_lambda, w_out, norm2_g, w_ffn_gate, w_ffn_up,
           w_ffn_down, final_norm_g):
    bp, s, _ = x_prompt.shape
    nb, nt, _ = x_sample.shape
    depth = norm1_g.shape[0]
    w_buf = cache_k.shape[2]
    assert bp == 1 and s % ATT_BLOCK == 0 and w_buf == MAX_WINDOW and nt * nb == BAND

    xp = x_prompt.reshape(s, D_MODEL)
    xs = x_sample.transpose(1, 0, 2).reshape(nt * nb, D_MODEL)
    rope_p = _rope_tables(jnp.asarray(_class_major_positions(s, TM_FRONT)))
    rope_s = _rope_tables(PAST_LEN + jnp.repeat(jnp.arange(nt, dtype=jnp.int32), nb))
    ck_t = cache_k.transpose(0, 1, 3, 4, 2)
    cv_t = cache_v.transpose(0, 1, 3, 4, 2)
    cst = state_conv.transpose(0, 2, 1, 3)
    gf = final_norm_g.reshape(1, D_MODEL)
    row = lambda v: v.reshape(1, -1)

    pk, pv, pc, ph, sc, sh = [], [], [], [], [], []
    k_all = v_all = None
    w_in_b = w_in.astype(BF16)
    wo_b = w_out.astype(BF16)
    wg_b = w_ffn_gate.astype(BF16)
    wu_b = w_ffn_up.astype(BF16)
    wd_b = w_ffn_down.astype(BF16)
    for l in range(depth):
        gates = _gate_weights(w_gate_a[l], w_gate_x[l])
        common = (conv_w[l], row(conv_b[l]), gates, row(b_gate_a[l]), row(b_gate_x[l]),
                  row(lru_lambda[l]))
        final = l == depth - 1

        qkv1, qkv4, qkv16, rnn, k_last, v_last, tail, h_last = _front_call(
            xp, row(norm1_g[l]), w_in_b, rope_p, *common, layer=l)
        att = _attn_call(qkv1, qkv4, qkv16)
        xp = _ffn_call(xp, att, rnn, wo_b, row(norm2_g[l]), wg_b, wu_b, wd_b, gf,
                       layer=l, final_norm=final, tm=TM_FFN)
        pk.append(k_last.reshape(1, MAX_WINDOW, N_HEADS, HEAD_DIM))
        pv.append(v_last.reshape(1, MAX_WINDOW, N_HEADS, HEAD_DIM))
        pc.append(tail[:, SUBLANES - 1, :].reshape(1, CONV_WIDTH - 1, D_RNN))
        ph.append(h_last[0:1])

        q_s, k_s, v_s, rnn_s, nconv, hl = _sample_front_call(
            xs, row(norm1_g[l]), w_in_b, rope_s, *common, cst[l], state_h[l],
            layer=l, nb=nb, nt=nt)

        def split(v):
            return v.reshape(nt, nb, N_HEADS, HEAD_DIM)

        q_b = jnp.pad(split(q_s).transpose(1, 2, 0, 3), ((0, 0), (0, 0), (0, SUBLANES - nt), (0, 0)))

        def fresh(v):
            return jnp.pad(v.reshape(nt, nb, D_ATT).transpose(1, 0, 2),
                           ((0, 0), (0, SUBLANES - nt), (0, 0)))

        k_all, v_all, att_s = _sample_attn_call(
            q_b, ck_t, cv_t, fresh(k_s), fresh(v_s), k_all, v_all,
            layer=l, depth=depth, nt=nt)
        att_s = att_s[:, :, :nt].transpose(2, 0, 1, 3).reshape(nt * nb, D_ATT).astype(BF16)
        xs = _ffn_call(xs, att_s, rnn_s, wo_b, row(norm2_g[l]), wg_b, wu_b, wd_b, gf,
                       layer=l, final_norm=final, tm=nt * nb)
        sc.append(nconv.transpose(1, 0, 2))
        sh.append(hl)

    y_prompt = xp.reshape(1, s, D_MODEL)
    y_sample = xs.reshape(nt, nb, D_MODEL).transpose(1, 0, 2)
    sample_k = k_all.transpose(0, 1, 4, 2, 3)
    sample_v = v_all.transpose(0, 1, 4, 2, 3)
    return (y_prompt, y_sample, jnp.stack(pk), jnp.stack(pv), jnp.stack(pc), jnp.stack(ph),
            sample_k, sample_v, jnp.stack(sc), jnp.stack(sh))
```

```python
import functools
import math

import numpy as np
import jax
import jax.numpy as jnp
from jax import lax
from jax.experimental import pallas as pl
from jax.experimental.pallas import tpu as pltpu

F32 = jnp.float32
BF16 = jnp.bfloat16

D_MODEL = 1024
HEAD_DIM = 64
D_ATT = 512
N_HEADS = 8
D_RNN = 512
N_RNN_BLOCKS = 8
CONV_WIDTH = 4
LRU_C = 8.0
DILATIONS = (1, 4, 16)
BAND = 128
MAX_WINDOW = 2048
ROPE_THETA = 500000.0
ROT_DIM = 16
D_FF = 2816
D_IN = 3 * D_ATT + 2 * D_RNN
RMS_EPS = 1e-6
PAST_LEN = 16384

LANES = 128
SUBLANES = 8
N_SLABS = D_ATT // LANES
FF_CHUNK = 256
NEG_BIG = -1e30
VMEM_LIMIT = 56 * 1024 * 1024

TM_FRONT = 512
TM_FFN = 512
ATT_BLOCK = MAX_WINDOW
ATT_STEPS = ATT_BLOCK // BAND
ATT_SUB = 2
CACHE_HEADS = 8
QKV_SLOT = (2, 0, 1)


def _dot(a, b):
    return jnp.dot(a, b, preferred_element_type=F32)


def _dot_nt(a, b):
    return lax.dot_general(a, b, (((1,), (1,)), ((), ())), preferred_element_type=F32)


def _rmsnorm(x, g):
    return x * lax.rsqrt(jnp.mean(x * x, axis=-1, keepdims=True) + RMS_EPS) * g


def _softplus(x):
    return jnp.maximum(x, 0.0) + jnp.log1p(jnp.exp(-jnp.abs(x)))


def _gelu_tanh(x):
    return 0.5 * x * (1.0 + jnp.tanh(math.sqrt(2.0 / math.pi) * (x + 0.044715 * (x * x * x))))


def _rope_slab(x, c, t):
    half = ROT_DIM // 2
    lane = lax.broadcasted_iota(jnp.int32, x.shape, 1)
    partner = jnp.where((lane & half) == 0, pltpu.roll(x, LANES - half, 1), pltpu.roll(x, half, 1))
    return x * c + partner * t


def _lru_coeffs(xc, wg_ref, bga, bgx, lam):
    xb = xc.astype(BF16)
    half = D_RNN // 2
    g0 = _dot(xb[:, :half], wg_ref[0])
    g1 = _dot(xb[:, half:], wg_ref[1])
    r = jax.nn.sigmoid(jnp.concatenate([g0[:, :half], g1[:, :half]], axis=1) + bga)
    ig = jax.nn.sigmoid(jnp.concatenate([g0[:, half:], g1[:, half:]], axis=1) + bgx)
    log_a = (-LRU_C) * r * _softplus(-lam)
    a = jnp.exp(log_a)
    th = jnp.tanh(log_a)
    b = jnp.sqrt(-2.0 * th / (1.0 - th)) * (ig * xc)
    return a, b


def _shift_rows(x, k, fill):
    row = lax.broadcasted_iota(jnp.int32, x.shape, 0)
    return jnp.where(row >= k, pltpu.roll(x, k, 0), fill)


def _front_kernel(x_ref, g1_ref, w_ref, cos_ref, sin_ref, cw_ref, cb_ref, wg_ref,
                  bga_ref, bgx_ref, lam_ref,
                  qkv1_ref, qkv4_ref, qkv16_ref, rnn_ref, klast_ref, vlast_ref, tail_ref, hlast_ref,
                  xs_s, hn_s, cls_s, nat_s, rn_s, carry_s, h_s, *, tm):
    i = pl.program_id(0)
    grp = tm // 4
    n16 = tm // 16

    @pl.when(i == 0)
    def _():
        carry_s[...] = jnp.zeros_like(carry_s)
        h_s[...] = jnp.zeros_like(h_s)

    hn32 = _rmsnorm(x_ref[...], g1_ref[...])
    n_x = D_MODEL // LANES
    for c in range(n_x):
        xs_s[c] = hn32[:, c * LANES:(c + 1) * LANES]
    for j in range(4):
        for c in range(n_x):
            hn_s[j * grp:(j + 1) * grp, c * LANES:(c + 1) * LANES] = (
                xs_s[c, pl.ds(j, grp, stride=4), :].astype(BF16))

    xr = _dot(hn_s[...], w_ref[:, 3 * D_ATT:3 * D_ATT + D_RNN])
    yg = _dot(hn_s[...], w_ref[:, 3 * D_ATT + D_RNN:])
    slab = [xr[j * grp:(j + 1) * grp, :] for j in range(4)]
    back = [None] + [_shift_rows(slab[j], 1, carry_s[j - 1, SUBLANES - 1:SUBLANES, :])
                     for j in range(1, 4)]
    for j in range(1, 4):
        carry_s[j - 1] = slab[j][grp - SUBLANES:, :]
        tail_ref[j - 1] = slab[j][grp - SUBLANES:, :]
    w = [cw_ref[k:k + 1, :] for k in range(CONV_WIDTH)]
    taps = ((slab[0], back[3], back[2], back[1]),
            (slab[1], slab[0], back[3], back[2]),
            (slab[2], slab[1], slab[0], back[3]),
            (slab[3], slab[2], slab[1], slab[0]))
    xc = jnp.concatenate(
        [cb_ref[...] + t[0] * w[3] + t[1] * w[2] + t[2] * w[1] + t[3] * w[0] for t in taps], axis=0)
    a, b = _lru_coeffs(xc, wg_ref, bga_ref[...], bgx_ref[...], lam_ref[...])

    pp = [a[0:grp, :]]
    hh = [b[0:grp, :]]
    for j in range(1, 4):
        aj = a[j * grp:(j + 1) * grp, :]
        hh.append(aj * hh[j - 1] + b[j * grp:(j + 1) * grp, :])
        pp.append(aj * pp[j - 1])
    pc, hc = pp[3], hh[3]
    row = lax.broadcasted_iota(jnp.int32, (grp, D_RNN), 0)
    k = 1
    while k < grp:
        keep = row >= k
        hc = jnp.where(keep, hc + pc * pltpu.roll(hc, k, 0), hc)
        pc = jnp.where(keep, pc * pltpu.roll(pc, k, 0), pc)
        k *= 2
    h_in = h_s[...]
    ends = hc + pc * h_in
    h_prev = _shift_rows(ends, 1, h_in)
    h_s[...] = ends[grp - 1:grp, :]
    hlast_ref[...] = jnp.broadcast_to(ends[grp - 1:grp, :], (SUBLANES, D_RNN))
    for j in range(4):
        gated = (hh[j] + pp[j] * h_prev) * _gelu_tanh(yg[j * grp:(j + 1) * grp, :])
        for c in range(N_SLABS):
            rn_s[c, pl.ds(j, grp, stride=4), :] = gated[:, c * LANES:(c + 1) * LANES]
    for c in range(N_SLABS):
        rnn_ref[:, c * LANES:(c + 1) * LANES] = rn_s[c].astype(BF16)

    cos = cos_ref[...]
    sin = sin_ref[...]
    for part in range(3):
        z = _dot(hn_s[...], w_ref[:, part * D_ATT:(part + 1) * D_ATT])
        for c in range(N_SLABS):
            zc = z[:, c * LANES:(c + 1) * LANES]
            if part < 2:
                zc = _rope_slab(zc, cos, sin)
            if part == 0:
                zc = zc * (1.0 / math.sqrt(HEAD_DIM))
            n = QKV_SLOT[part] * N_SLABS + c
            cols = slice(n * LANES, (n + 1) * LANES)
            cls_s[n] = zc
            for j in range(4):
                rows = zc[j * grp:(j + 1) * grp, :]
                qkv4_ref[0, j, :, cols] = rows.astype(BF16)
                nat_s[n, pl.ds(j, grp, stride=4), :] = rows
            for j in range(4):
                for jj in range(4):
                    rows = cls_s[n, pl.ds(j * grp + jj, n16, stride=4), :]
                    qkv16_ref[0, j + 4 * jj, :, cols] = rows.astype(BF16)
            nat = nat_s[n]
            qkv1_ref[:, cols] = nat.astype(BF16)
            if part == 1:
                klast_ref[:, c * LANES:(c + 1) * LANES] = nat
            if part == 2:
                vlast_ref[:, c * LANES:(c + 1) * LANES] = nat


def _class_major_positions(s, tm):
    r = np.arange(s)
    tile, within = r // tm, r % tm
    grp = tm // 4
    return (tile * tm + 4 * (within % grp) + within // grp).astype(np.int32)


def _front_call(x, g1, w_in, rope, cw, cb, wg, bga, bgx, lam, *, layer):
    s = x.shape[0]
    tm = TM_FRONT
    nt = s // tm
    n_last = MAX_WINDOW // tm
    row = lambda i: (i, 0)
    const2 = lambda i: (0, 0)
    const3 = lambda i: (0, 0, 0)
    last = lambda i: (jnp.maximum(i - (nt - n_last), 0), 0)
    t4 = 4 * BAND // tm
    t16 = ATT_BLOCK // tm
    out_shape = (
        jax.ShapeDtypeStruct((s, 3 * D_ATT), BF16),
        jax.ShapeDtypeStruct((s // (4 * BAND), 4, BAND, 3 * D_ATT), BF16),
        jax.ShapeDtypeStruct((s // ATT_BLOCK, ATT_STEPS, BAND, 3 * D_ATT), BF16),
        jax.ShapeDtypeStruct((s, D_RNN), BF16),
        jax.ShapeDtypeStruct((MAX_WINDOW, D_ATT), F32),
        jax.ShapeDtypeStruct((MAX_WINDOW, D_ATT), F32),
        jax.ShapeDtypeStruct((CONV_WIDTH - 1, SUBLANES, D_RNN), F32),
        jax.ShapeDtypeStruct((SUBLANES, D_RNN), F32),
    )
    out_specs = (
        pl.BlockSpec((tm, 3 * D_ATT), row),
        pl.BlockSpec((1, 4, tm // 4, 3 * D_ATT), lambda i: (i // t4, 0, i % t4, 0)),
        pl.BlockSpec((1, ATT_STEPS, tm // ATT_STEPS, 3 * D_ATT), lambda i: (i // t16, 0, i % t16, 0)),
        pl.BlockSpec((tm, D_RNN), row),
        pl.BlockSpec((tm, D_ATT), last),
        pl.BlockSpec((tm, D_ATT), last),
        pl.BlockSpec((CONV_WIDTH - 1, SUBLANES, D_RNN), const3),
        pl.BlockSpec((SUBLANES, D_RNN), const2),
    )
    in_specs = [
        pl.BlockSpec((tm, D_MODEL), row),
        pl.BlockSpec((1, D_MODEL), const2),
        pl.BlockSpec((None, D_MODEL, D_IN), lambda i: (layer, 0, 0), pipeline_mode=pl.Buffered(1)),
        pl.BlockSpec((tm, LANES), lambda i: (i, 0)),
        pl.BlockSpec((tm, LANES), lambda i: (i, 1)),
        pl.BlockSpec((CONV_WIDTH, D_RNN), const2),
        pl.BlockSpec((1, D_RNN), const2),
        pl.BlockSpec((2, D_RNN // 2, D_RNN), const3),
        pl.BlockSpec((1, D_RNN), const2),
        pl.BlockSpec((1, D_RNN), const2),
        pl.BlockSpec((1, D_RNN), const2),
    ]
    scratch = [
        pltpu.VMEM((D_MODEL // LANES, tm, LANES), F32),
        pltpu.VMEM((tm, D_MODEL), BF16),
        pltpu.VMEM((3 * N_SLABS, tm, LANES), F32),
        pltpu.VMEM((3 * N_SLABS, tm, LANES), F32),
        pltpu.VMEM((N_SLABS, tm, LANES), F32),
        pltpu.VMEM((CONV_WIDTH - 1, SUBLANES, D_RNN), F32),
        pltpu.VMEM((1, D_RNN), F32),
    ]
    return pl.pallas_call(
        functools.partial(_front_kernel, tm=tm),
        out_shape=out_shape,
        grid=(nt,),
        in_specs=in_specs,
        out_specs=out_specs,
        scratch_shapes=scratch,
        compiler_params=pltpu.CompilerParams(
            dimension_semantics=("arbitrary",), vmem_limit_bytes=VMEM_LIMIT),
        name="prompt_front",
    )(x, g1, w_in, rope, rope, cw, cb, wg, bga, bgx, lam)


def _band_bias():
    a = np.arange(BAND)[:, None]
    c = np.arange(2 * BAND)[None, :]
    dist = BAND + a - c
    ok = (dist >= 0) & (dist <= BAND)
    with_prev = np.where(ok, 0.0, NEG_BIG)
    no_prev = np.where(ok & (c >= BAND), 0.0, NEG_BIG)
    return np.tile(np.stack([with_prev, no_prev]), (1, 2, 1)).astype(np.float32)


def _merge_rows(hp, rows, o_new, m_new, l_new, acc_s, m_s, l_s):
    m_old = m_s[hp, rows, :]
    m_tot = jnp.maximum(m_old, m_new)
    w_old = jnp.exp(m_old - m_tot)
    w_new = jnp.exp(m_new - m_tot)
    m_s[hp, rows, :] = m_tot
    l_s[hp, rows, :] = l_s[hp, rows, :] * w_old + l_new * w_new
    acc_s[hp, rows, :] = acc_s[hp, rows, :] * w_old + o_new * w_new


def _attn_kernel(bias_ref, *refs):
    pat_refs = [refs[3 * g:3 * g + 3] for g in range(3)]
    o_ref = refs[9]
    acc_s, m_s, l_s, tmp_s, nat_s = refs[10:15]
    blk = pl.program_id(0)
    step = pl.program_id(1)
    quarter = ATT_BLOCK // 4
    piece = BAND // 4

    @pl.when(step == 0)
    def _():
        acc_s[...] = jnp.zeros_like(acc_s)
        l_s[...] = jnp.zeros_like(l_s)
        m_s[...] = jnp.full(m_s.shape, NEG_BIG, F32)

    lane = lax.broadcasted_iota(jnp.int32, (BAND, LANES), 1)
    low = lane < HEAD_DIM

    for sub in range(ATT_SUB):
        s = step * ATT_SUB + sub
        first = (blk * ATT_STEPS + s == 0, blk * 4 + s // 4 == 0, blk == 0)
        base4 = (s % 4) * quarter
        rows = (None,
                pl.ds(pl.multiple_of(base4 + (s // 4) * BAND, BAND), BAND),
                pl.ds(base4 + s // 4, BAND, stride=4))
        for g in range(3):
            q_ref, prev_ref, cur_ref = pat_refs[g]
            bias = bias_ref[jnp.where(first[g], 1, 0)]
            for hp in range(N_SLABS):
                kcols = slice(hp * LANES, (hp + 1) * LANES)
                vcols = slice(D_ATT + hp * LANES, D_ATT + (hp + 1) * LANES)
                if g == 0:
                    here = slice(sub * BAND, (sub + 1) * BAND)
                    before = slice((sub - 1) * BAND, sub * BAND)
                    q2 = q_ref[here, kcols]
                    kp, vp = ((prev_ref[:, kcols], prev_ref[:, vcols]) if sub == 0 else
                              (cur_ref[before, kcols], cur_ref[before, vcols]))
                    kc, vc = cur_ref[here, kcols], cur_ref[here, vcols]
                else:
                    q2 = q_ref[sub, :, kcols]
                    kp, vp = prev_ref[sub, :, kcols], prev_ref[sub, :, vcols]
                    kc, vc = cur_ref[sub, :, kcols], cur_ref[sub, :, vcols]
                k2 = jnp.concatenate([kp, kc], axis=0)
                v2 = jnp.concatenate([vp, vc], axis=0)
                zero = jnp.zeros_like(q2)
                qq = jnp.concatenate([jnp.where(low, q2, zero), jnp.where(low, zero, q2)], axis=0)
                sc = _dot_nt(qq, k2) + bias
                mx = jnp.max(sc, axis=1, keepdims=True)
                p = jnp.exp(sc - mx)
                sm = jnp.sum(p, axis=1, keepdims=True)
                oo = _dot(p.astype(BF16), v2)
                o_new = jnp.where(low, oo[:BAND], oo[BAND:])
                m_new = jnp.where(low, mx[:BAND], mx[BAND:])
                l_new = jnp.where(low, sm[:BAND], sm[BAND:])
                if g == 0:
                    slot = 3 * (sub * N_SLABS + hp)
                    for n, val in enumerate((o_new, m_new, l_new)):
                        tmp_s[slot + n] = val
                    for c4 in range(4):
                        pick = pl.ds(c4, piece, stride=4)
                        dst = pl.ds(pl.multiple_of(c4 * quarter + s * piece, piece), piece)
                        _merge_rows(hp, dst, tmp_s[slot, pick, :], tmp_s[slot + 1, pick, :],
                                    tmp_s[slot + 2, pick, :], acc_s, m_s, l_s)
                else:
                    _merge_rows(hp, rows[g], o_new, m_new, l_new, acc_s, m_s, l_s)

    @pl.when(step == pl.num_programs(1) - 1)
    def _():
        for hp in range(N_SLABS):
            for c4 in range(4):
                part = slice(c4 * quarter, (c4 + 1) * quarter)
                nat_s[pl.ds(c4, quarter, stride=4), :] = acc_s[hp, part, :] / l_s[hp, part, :]
            o_ref[:, hp * LANES:(hp + 1) * LANES] = nat_s[...].astype(BF16)


def _attn_call(qkv1, qkv4, qkv16):
    s = qkv1.shape[0]
    nb = s // ATT_BLOCK
    steps = ATT_STEPS // ATT_SUB
    per4 = 4 // ATT_SUB
    wide = 3 * D_ATT
    q1 = qkv1.reshape(s // BAND, BAND, wide)
    q1_sub = qkv1.reshape(s // (ATT_SUB * BAND), ATT_SUB * BAND, wide)

    def width(is_q):
        return D_ATT if is_q else 2 * D_ATT

    def col(is_q):
        return QKV_SLOT[0] if is_q else 0

    def spec1(is_q, prev):
        if prev:
            return pl.BlockSpec(
                (None, BAND, width(is_q)),
                lambda b, t: (jnp.maximum((b * steps + t) * ATT_SUB - 1, 0), 0, col(is_q)))
        return pl.BlockSpec((None, ATT_SUB * BAND, width(is_q)),
                            lambda b, t: (b * steps + t, 0, col(is_q)))

    def spec4(is_q, prev):
        def im(b, t):
            j = b * 4 + t // per4
            return (jnp.maximum(j - 1, 0) if prev else j, t % per4, 0, col(is_q))
        return pl.BlockSpec((None, ATT_SUB, BAND, width(is_q)), im)

    def spec16(is_q, prev):
        def im(b, t):
            return (jnp.maximum(b - 1, 0) if prev else b, t, 0, col(is_q))
        return pl.BlockSpec((None, ATT_SUB, BAND, width(is_q)), im)

    in_specs = [pl.BlockSpec((2, 2 * BAND, 2 * BAND), lambda b, t: (0, 0, 0))]
    args = [jnp.asarray(_band_bias())]
    in_specs += [spec1(True, False), spec1(False, True), spec1(False, False)]
    args += [q1_sub, q1, q1_sub]
    for arr, mk in ((qkv4, spec4), (qkv16, spec16)):
        in_specs += [mk(True, False), mk(False, True), mk(False, False)]
        args += [arr] * 3
    state = pltpu.VMEM((N_SLABS, ATT_BLOCK, LANES), F32)
    return pl.pallas_call(
        _attn_kernel,
        out_shape=jax.ShapeDtypeStruct((s, D_ATT), BF16),
        grid=(nb, steps),
        in_specs=in_specs,
        out_specs=pl.BlockSpec((ATT_BLOCK, D_ATT), lambda b, t: (b, 0)),
        scratch_shapes=[state, state, state,
                        pltpu.VMEM((3 * ATT_SUB * N_SLABS, BAND, LANES), F32),
                        pltpu.VMEM((ATT_BLOCK, LANES), F32)],
        compiler_params=pltpu.CompilerParams(
            dimension_semantics=("arbitrary", "arbitrary"), vmem_limit_bytes=VMEM_LIMIT),
        name="prompt_attention",
    )(*args)


def _ffn_kernel(x_ref, att_ref, rnn_ref, wo_ref, g2_ref, wg_ref, wu_ref, wd_ref, gf_ref,
                o_ref, act_s, *, final_norm):
    y = (x_ref[...] + _dot(att_ref[...], wo_ref[0:D_ATT, :])
         + _dot(rnn_ref[...], wo_ref[D_ATT:D_MODEL, :]))
    hf = _rmsnorm(y, g2_ref[...]).astype(BF16)
    for c in range(D_FF // FF_CHUNK):
        cols = slice(c * FF_CHUNK, (c + 1) * FF_CHUNK)
        gate = _dot(hf, wg_ref[:, cols])
        up = _dot(hf, wu_ref[:, cols])
        act_s[:, cols] = (gate * jax.nn.sigmoid(gate) * up).astype(BF16)
    out = y + _dot(act_s[...], wd_ref[...])
    if final_norm:
        out = _rmsnorm(out, gf_ref[...])
    o_ref[...] = out


def _ffn_call(x, att, rnn, wo, g2, wg, wu, wd, gf, *, layer, final_norm, tm):
    s = x.shape[0]
    row = lambda i: (i, 0)
    const = lambda i: (0, 0)
    pick = lambda i: (layer, 0, 0)
    once = pl.Buffered(1)
    in_specs = [
        pl.BlockSpec((tm, D_MODEL), row),
        pl.BlockSpec((tm, D_ATT), row),
        pl.BlockSpec((tm, D_RNN), row),
        pl.BlockSpec((None, D_MODEL, D_MODEL), pick, pipeline_mode=once),
        pl.BlockSpec((1, D_MODEL), const),
        pl.BlockSpec((None, D_MODEL, D_FF), pick, pipeline_mode=once),
        pl.BlockSpec((None, D_MODEL, D_FF), pick, pipeline_mode=once),
        pl.BlockSpec((None, D_FF, D_MODEL), pick, pipeline_mode=once),
        pl.BlockSpec((1, D_MODEL), const),
    ]
    return pl.pallas_call(
        functools.partial(_ffn_kernel, final_norm=final_norm),
        out_shape=jax.ShapeDtypeStruct((s, D_MODEL), F32),
        grid=(s // tm,),
        in_specs=in_specs,
        out_specs=pl.BlockSpec((tm, D_MODEL), row),
        scratch_shapes=[pltpu.VMEM((tm, D_FF), BF16)],
        compiler_params=pltpu.CompilerParams(
            dimension_semantics=("arbitrary",), vmem_limit_bytes=VMEM_LIMIT),
        name="out_proj_ffn",
    )(x, att, rnn, wo, g2, wg, wu, wd, gf)


def _sample_front_kernel(x_ref, g1_ref, w_ref, cos_ref, sin_ref, cw_ref, cb_ref, wg_ref,
                         bga_ref, bgx_ref, lam_ref, cst_ref, h0_ref,
                         q_ref, k_ref, v_ref, rnn_ref, nconv_ref, hl_ref, *, nb, nt):
    hn = _rmsnorm(x_ref[...], g1_ref[...]).astype(BF16)
    cos = cos_ref[...]
    sin = sin_ref[...]
    for part, out in enumerate((q_ref, k_ref, v_ref)):
        z = _dot(hn, w_ref[:, part * D_ATT:(part + 1) * D_ATT])
        for c in range(N_SLABS):
            zc = z[:, c * LANES:(c + 1) * LANES]
            if part < 2:
                zc = _rope_slab(zc, cos, sin)
            if part == 0:
                zc = zc * (1.0 / math.sqrt(HEAD_DIM))
            out[:, c * LANES:(c + 1) * LANES] = zc
    xr = _dot(hn, w_ref[:, 3 * D_ATT:3 * D_ATT + D_RNN])
    yg = _dot(hn, w_ref[:, 3 * D_ATT + D_RNN:])
    xpad = [cst_ref[j] for j in range(CONV_WIDTH - 1)]
    xpad += [xr[t * nb:(t + 1) * nb, :] for t in range(nt)]
    xc = []
    for t in range(nt):
        acc = cb_ref[...] + xpad[t] * cw_ref[0:1, :]
        for j in range(1, CONV_WIDTH):
            acc = acc + xpad[t + j] * cw_ref[j:j + 1, :]
        xc.append(acc)
    for j in range(CONV_WIDTH - 1):
        nconv_ref[j] = xpad[nt + j]
    a, b = _lru_coeffs(jnp.concatenate(xc, axis=0), wg_ref, bga_ref[...], bgx_ref[...],
                       lam_ref[...])
    h = h0_ref[...]
    hs = []
    for t in range(nt):
        h = a[t * nb:(t + 1) * nb, :] * h + b[t * nb:(t + 1) * nb, :]
        hs.append(h)
    hl_ref[...] = h
    rnn_ref[...] = (jnp.concatenate(hs, axis=0) * _gelu_tanh(yg)).astype(BF16)


def _sample_front_call(x, g1, w_in, rope, cw, cb, wg, bga, bgx, lam, cst, h0, *, layer, nb, nt):
    m = x.shape[0]

    def whole(a):
        return pl.BlockSpec(a.shape, lambda i, nd=a.ndim: (0,) * nd)

    in_specs = [whole(x), whole(g1),
                pl.BlockSpec((None, D_MODEL, D_IN), lambda i: (layer, 0, 0)),
                pl.BlockSpec((m, LANES), lambda i: (0, 0)), pl.BlockSpec((m, LANES), lambda i: (0, 1))]
    in_specs += [whole(a) for a in (cw, cb, wg, bga, bgx, lam, cst, h0)]
    out_shape = (
        jax.ShapeDtypeStruct((m, D_ATT), F32),
        jax.ShapeDtypeStruct((m, D_ATT), F32),
        jax.ShapeDtypeStruct((m, D_ATT), F32),
        jax.ShapeDtypeStruct((m, D_RNN), BF16),
        jax.ShapeDtypeStruct((CONV_WIDTH - 1, nb, D_RNN), F32),
        jax.ShapeDtypeStruct((nb, D_RNN), F32),
    )
    out_specs = tuple(pl.BlockSpec(o.shape, lambda i, nd=len(o.shape): (0,) * nd) for o in out_shape)
    return pl.pallas_call(
        functools.partial(_sample_front_kernel, nb=nb, nt=nt),
        out_shape=out_shape,
        grid=(1,),
        in_specs=in_specs,
        out_specs=out_specs,
        compiler_params=pltpu.CompilerParams(
            dimension_semantics=("arbitrary",), vmem_limit_bytes=VMEM_LIMIT),
        name="sample_front",
    )(x, g1, w_in, rope, rope, cw, cb, wg, bga, bgx, lam, cst, h0)


def _sample_mult(nt, w_buf):
    cache = np.zeros((SUBLANES, w_buf), np.float32)
    new = np.zeros((SUBLANES, LANES), np.float32)
    for dil in DILATIONS:
        for m in range(BAND + 1):
            idx = w_buf + np.arange(nt) - dil * m
            for t in range(nt):
                if idx[t] < 0:
                    continue
                if idx[t] < w_buf:
                    cache[t, idx[t]] += 1.0
                else:
                    new[t, LANES - nt + (idx[t] - w_buf)] += 1.0
    cache[nt:] = 1.0
    return cache, new


def _sample_attn_kernel(q_ref, kc_ref, vc_ref, kn_ref, vn_ref, mc_ref, mn_ref, *rest, nt, aliased):
    if aliased:
        rest = rest[2:]
    ko_ref, vo_ref, att_ref = rest
    w_buf = kc_ref.shape[-1]
    mc = mc_ref[...]
    mn = mn_ref[...]
    lane = lax.broadcasted_iota(jnp.int32, (HEAD_DIM, LANES), 1)
    fresh = lane >= LANES - nt

    def tail_tiles(new_ref, pair):
        rows = jnp.concatenate([new_ref[0, :, pair * LANES:(pair + 1) * LANES],
                                jnp.zeros((LANES - SUBLANES, LANES), F32)], axis=0)
        placed = pltpu.roll(rows.T, LANES - nt, 1)
        return placed[:HEAD_DIM], placed[HEAD_DIM:]

    kn_tiles, vn_tiles = [], []
    for pair in range(CACHE_HEADS // 2):
        kn_tiles += tail_tiles(kn_ref, pair)
        vn_tiles += tail_tiles(vn_ref, pair)
    for h in range(CACHE_HEADS):
        q = q_ref[0, h].astype(BF16)
        kc = kc_ref[0, 0, h]
        vc = vc_ref[0, 0, h]
        kn = kn_tiles[h]
        vn = vn_tiles[h]
        sc = jnp.where(mc > 0, _dot(q, kc.astype(BF16)), NEG_BIG)
        sn = jnp.where(mn > 0, _dot(q, kn.astype(BF16)), NEG_BIG)
        mx = jnp.maximum(jnp.max(sc, axis=1, keepdims=True), jnp.max(sn, axis=1, keepdims=True))
        pc = mc * jnp.exp(sc - mx)
        pn = mn * jnp.exp(sn - mx)
        den = jnp.sum(pc, axis=1, keepdims=True) + jnp.sum(pn, axis=1, keepdims=True)
        o = _dot_nt(pc.astype(BF16), vc.astype(BF16)) + _dot_nt(pn.astype(BF16), vn.astype(BF16))
        att_ref[0, h] = o / den
        for src, new, dst in ((kc, kn, ko_ref), (vc, vn, vo_ref)):
            moved = pltpu.roll(src, w_buf - nt, 1)
            dst[0, 0, h, :, 0:w_buf - LANES] = moved[:, 0:w_buf - LANES]
            dst[0, 0, h, :, w_buf - LANES:] = jnp.where(fresh, new, moved[:, w_buf - LANES:])


def _sample_attn_call(q, kc, vc, kn, vn, k_all, v_all, *, layer, depth, nt):
    _, nb, nh, _, w_buf = kc.shape
    hb = CACHE_HEADS
    mc, mn = _sample_mult(nt, w_buf)
    blk_c = pl.BlockSpec((1, 1, hb, HEAD_DIM, w_buf), lambda b, h: (layer, b, h, 0, 0))
    blk_n = pl.BlockSpec((1, SUBLANES, hb * HEAD_DIM), lambda b, h: (b, 0, h))
    blk_q = pl.BlockSpec((1, hb, SUBLANES, HEAD_DIM), lambda b, h: (b, h, 0, 0))
    blk_o = pl.BlockSpec((1, 1, hb, HEAD_DIM, w_buf), lambda b, h: (layer, b, h, 0, 0))
    const = lambda b, h: (0, 0)
    in_specs = [blk_q, blk_c, blk_c, blk_n, blk_n,
                pl.BlockSpec((SUBLANES, w_buf), const), pl.BlockSpec((SUBLANES, LANES), const)]
    args = [q, kc, vc, kn, vn, jnp.asarray(mc), jnp.asarray(mn)]
    aliases = {}
    aliased = k_all is not None
    if aliased:
        in_specs += [pl.BlockSpec(memory_space=pl.ANY), pl.BlockSpec(memory_space=pl.ANY)]
        args += [k_all, v_all]
        aliases = {7: 0, 8: 1}
    full = jax.ShapeDtypeStruct((depth, nb, nh, HEAD_DIM, w_buf), F32)
    return pl.pallas_call(
        functools.partial(_sample_attn_kernel, nt=nt, aliased=aliased),
        out_shape=(full, full, jax.ShapeDtypeStruct((nb, nh, SUBLANES, HEAD_DIM), F32)),
        grid=(nb, nh // hb),
        in_specs=in_specs,
        out_specs=(blk_o, blk_o, blk_q),
        input_output_aliases=aliases,
        compiler_params=pltpu.CompilerParams(
            dimension_semantics=("arbitrary", "arbitrary"), vmem_limit_bytes=VMEM_LIMIT),
        name="sample_attention",
    )(*args)


def _rope_tables(pos):
    half = ROT_DIM // 2
    inv = ROPE_THETA ** (-np.arange(half, dtype=np.float64) * 2.0 / ROT_DIM)
    ang = np.asarray(pos, np.float64)[:, None] * inv[None, :]
    cos, sin = np.cos(ang), np.sin(ang)
    n = ang.shape[0]
    rest = HEAD_DIM - ROT_DIM
    c = np.concatenate([cos, cos, np.ones((n, rest))], axis=1)
    t = np.concatenate([-sin, sin, np.zeros((n, rest))], axis=1)
    rep = LANES // HEAD_DIM
    return jnp.asarray(np.concatenate([np.tile(c, (1, rep)), np.tile(t, (1, rep))], axis=1), F32)


def _gate_weights(w_a, w_x):
    def dense(w):
        return jax.scipy.linalg.block_diag(*[w[n] for n in range(N_RNN_BLOCKS)])
    da, dx = dense(w_a), dense(w_x)
    half = D_RNN // 2
    chunks = [jnp.concatenate([da[c * half:(c + 1) * half, c * half:(c + 1) * half],
                               dx[c * half:(c + 1) * half, c * half:(c + 1) * half]], axis=1)
              for c in range(2)]
    return jnp.stack(chunks).astype(BF16)


def kernel(x_prompt, x_sample, cache_k, cache_v, state_conv, state_h, norm1_g, w_in, conv_w, conv_b,
           w_gate_a, b_gate_a, w_gate_x, b_gate_x, lru_lambda, w_out, norm2_g, w_ffn_gate, w_ffn_up,
           w_ffn_down, final_norm_g):
    bp, s, _ = x_prompt.shape
    nb, nt, _ = x_sample.shape
    depth = norm1_g.shape[0]
    w_buf = cache_k.shape[2]
    assert bp == 1 and s % ATT_BLOCK == 0 and w_buf == MAX_WINDOW and nt * nb == BAND

    xp = x_prompt.reshape(s, D_MODEL)
    xs = x_sample.transpose(1, 0, 2).reshape(nt * nb, D_MODEL)
    rope_p = _rope_tables(_class_major_positions(s, TM_FRONT))
    rope_s = _rope_tables(PAST_LEN + np.repeat(np.arange(nt), nb))
    ck_t = cache_k.transpose(0, 1, 3, 4, 2)
    cv_t = cache_v.transpose(0, 1, 3, 4, 2)
    cst = state_conv.transpose(0, 2, 1, 3)
    gf = final_norm_g.reshape(1, D_MODEL)
    row = lambda v: v.reshape(1, -1)

    pk, pv, pc, ph, sc, sh = [], [], [], [], [], []
    k_all = v_all = None
    w_in_b = w_in.astype(BF16)
    wo_b = w_out.astype(BF16)
    wg_b = w_ffn_gate.astype(BF16)
    wu_b = w_ffn_up.astype(BF16)
    wd_b = w_ffn_down.astype(BF16)
    for l in range(depth):
        gates = _gate_weights(w_gate_a[l], w_gate_x[l])
        common = (conv_w[l], row(conv_b[l]), gates, row(b_gate_a[l]), row(b_gate_x[l]),
                  row(lru_lambda[l]))
        final = l == depth - 1

        qkv1, qkv4, qkv16, rnn, k_last, v_last, tail, h_last = _front_call(
            xp, row(norm1_g[l]), w_in_b, rope_p, *common, layer=l)
        att = _attn_call(qkv1, qkv4, qkv16)
        xp = _ffn_call(xp, att, rnn, wo_b, row(norm2_g[l]), wg_b, wu_b, wd_b, gf,
                       layer=l, final_norm=final, tm=TM_FFN)
        pk.append(k_last.reshape(1, MAX_WINDOW, N_HEADS, HEAD_DIM))
        pv.append(v_last.reshape(1, MAX_WINDOW, N_HEADS, HEAD_DIM))
        pc.append(tail[:, SUBLANES - 1, :].reshape(1, CONV_WIDTH - 1, D_RNN))
        ph.append(h_last[0:1])

        q_s, k_s, v_s, rnn_s, nconv, hl = _sample_front_call(
            xs, row(norm1_g[l]), w_in_b, rope_s, *common, cst[l], state_h[l],
            layer=l, nb=nb, nt=nt)

        def split(v):
            return v.reshape(nt, nb, N_HEADS, HEAD_DIM)

        q_b = jnp.pad(split(q_s).transpose(1, 2, 0, 3), ((0, 0), (0, 0), (0, SUBLANES - nt), (0, 0)))

        def fresh(v):
            return jnp.pad(v.reshape(nt, nb, D_ATT).transpose(1, 0, 2),
                           ((0, 0), (0, SUBLANES - nt), (0, 0)))

        k_all, v_all, att_s = _sample_attn_call(
            q_b, ck_t, cv_t, fresh(k_s), fresh(v_s), k_all, v_all,
            layer=l, depth=depth, nt=nt)
        att_s = att_s[:, :, :nt].transpose(2, 0, 1, 3).reshape(nt * nb, D_ATT).astype(BF16)
        xs = _ffn_call(xs, att_s, rnn_s, wo_b, row(norm2_g[l]), wg_b, wu_b, wd_b, gf,
                       layer=l, final_norm=final, tm=nt * nb)
        sc.append(nconv.transpose(1, 0, 2))
        sh.append(hl)

    y_prompt = xp.reshape(1, s, D_MODEL)
    y_sample = xs.reshape(nt, nb, D_MODEL).transpose(1, 0, 2)
    sample_k = k_all.transpose(0, 1, 4, 2, 3)
    sample_v = v_all.transpose(0, 1, 4, 2, 3)
    return (y_prompt, y_sample, jnp.stack(pk), jnp.stack(pv), jnp.stack(pc), jnp.stack(ph),
            sample_k, sample_v, jnp.stack(sc), jnp.stack(sh))
```

```python
import functools
import math

import numpy as np
import jax
import jax.numpy as jnp
from jax import lax
from jax.experimental import pallas as pl
from jax.experimental.pallas import tpu as pltpu

F32 = jnp.float32
BF16 = jnp.bfloat16

D_MODEL = 1024
HEAD_DIM = 64
D_ATT = 512
N_HEADS = 8
D_RNN = 512
N_RNN_BLOCKS = 8
CONV_WIDTH = 4
LRU_C = 8.0
DILATIONS = (1, 4, 16)
BAND = 128
MAX_WINDOW = 2048
ROPE_THETA = 500000.0
ROT_DIM = 16
D_FF = 2816
D_IN = 3 * D_ATT + 2 * D_RNN
RMS_EPS = 1e-6
PAST_LEN = 16384

LANES = 128
SUBLANES = 8
N_SLABS = D_ATT // LANES
FF_CHUNK = 256
NEG_BIG = -1e30
VMEM_LIMIT = 56 * 1024 * 1024

TM_FRONT = 512
TM_FFN = 1024
ATT_BLOCK = MAX_WINDOW
ATT_STEPS = ATT_BLOCK // BAND
ATT_SUB = 2
CACHE_HEADS = 8
QKV_SLOT = (2, 0, 1)


def _dot(a, b):
    return jnp.dot(a, b, preferred_element_type=F32)


def _dot_nt(a, b):
    return lax.dot_general(a, b, (((1,), (1,)), ((), ())), preferred_element_type=F32)


def _rmsnorm(x, g):
    return x * lax.rsqrt(jnp.mean(x * x, axis=-1, keepdims=True) + RMS_EPS) * g


def _softplus(x):
    return jnp.maximum(x, 0.0) + jnp.log1p(jnp.exp(-jnp.abs(x)))


def _gelu_tanh(x):
    return 0.5 * x * (1.0 + jnp.tanh(math.sqrt(2.0 / math.pi) * (x + 0.044715 * (x * x * x))))


def _rope_slab(x, c, t):
    half = ROT_DIM // 2
    lane = lax.broadcasted_iota(jnp.int32, x.shape, 1)
    partner = jnp.where((lane & half) == 0, pltpu.roll(x, LANES - half, 1), pltpu.roll(x, half, 1))
    return x * c + partner * t


def _lru_coeffs(xc, wg_ref, bga, bgx, lam):
    xb = xc.astype(BF16)
    half = D_RNN // 2
    g0 = _dot(xb[:, :half], wg_ref[0])
    g1 = _dot(xb[:, half:], wg_ref[1])
    r = jax.nn.sigmoid(jnp.concatenate([g0[:, :half], g1[:, :half]], axis=1) + bga)
    ig = jax.nn.sigmoid(jnp.concatenate([g0[:, half:], g1[:, half:]], axis=1) + bgx)
    log_a = (-LRU_C) * r * _softplus(-lam)
    a = jnp.exp(log_a)
    th = jnp.tanh(log_a)
    b = jnp.sqrt(-2.0 * th / (1.0 - th)) * (ig * xc)
    return a, b


def _shift_rows(x, k, fill):
    row = lax.broadcasted_iota(jnp.int32, x.shape, 0)
    return jnp.where(row >= k, pltpu.roll(x, k, 0), fill)


def _front_kernel(x_ref, g1_ref, w_ref, cos_ref, sin_ref, cw_ref, cb_ref, wg_ref,
                  bga_ref, bgx_ref, lam_ref,
                  qkv1_ref, qkv4_ref, qkv16_ref, rnn_ref, klast_ref, vlast_ref, tail_ref, hlast_ref,
                  xs_s, hn_s, cls_s, nat_s, rn_s, carry_s, h_s, *, tm):
    i = pl.program_id(0)
    grp = tm // 4
    n16 = tm // 16

    @pl.when(i == 0)
    def _():
        carry_s[...] = jnp.zeros_like(carry_s)
        h_s[...] = jnp.zeros_like(h_s)

    hn32 = _rmsnorm(x_ref[...], g1_ref[...])
    n_x = D_MODEL // LANES
    for c in range(n_x):
        xs_s[c] = hn32[:, c * LANES:(c + 1) * LANES]
    for j in range(4):
        for c in range(n_x):
            hn_s[j * grp:(j + 1) * grp, c * LANES:(c + 1) * LANES] = (
                xs_s[c, pl.ds(j, grp, stride=4), :].astype(BF16))

    xr = _dot(hn_s[...], w_ref[:, 3 * D_ATT:3 * D_ATT + D_RNN])
    yg = _dot(hn_s[...], w_ref[:, 3 * D_ATT + D_RNN:])
    slab = [xr[j * grp:(j + 1) * grp, :] for j in range(4)]
    back = [None] + [_shift_rows(slab[j], 1, carry_s[j - 1, SUBLANES - 1:SUBLANES, :])
                     for j in range(1, 4)]
    for j in range(1, 4):
        carry_s[j - 1] = slab[j][grp - SUBLANES:, :]
        tail_ref[j - 1] = slab[j][grp - SUBLANES:, :]
    w = [cw_ref[k:k + 1, :] for k in range(CONV_WIDTH)]
    taps = ((slab[0], back[3], back[2], back[1]),
            (slab[1], slab[0], back[3], back[2]),
            (slab[2], slab[1], slab[0], back[3]),
            (slab[3], slab[2], slab[1], slab[0]))
    xc = jnp.concatenate(
        [cb_ref[...] + t[0] * w[3] + t[1] * w[2] + t[2] * w[1] + t[3] * w[0] for t in taps], axis=0)
    a, b = _lru_coeffs(xc, wg_ref, bga_ref[...], bgx_ref[...], lam_ref[...])

    pp = [a[0:grp, :]]
    hh = [b[0:grp, :]]
    for j in range(1, 4):
        aj = a[j * grp:(j + 1) * grp, :]
        hh.append(aj * hh[j - 1] + b[j * grp:(j + 1) * grp, :])
        pp.append(aj * pp[j - 1])
    pc, hc = pp[3], hh[3]
    row = lax.broadcasted_iota(jnp.int32, (grp, D_RNN), 0)
    k = 1
    while k < grp:
        keep = row >= k
        hc = jnp.where(keep, hc + pc * pltpu.roll(hc, k, 0), hc)
        pc = jnp.where(keep, pc * pltpu.roll(pc, k, 0), pc)
        k *= 2
    h_in = h_s[...]
    ends = hc + pc * h_in
    h_prev = _shift_rows(ends, 1, h_in)
    h_s[...] = ends[grp - 1:grp, :]
    hlast_ref[...] = jnp.broadcast_to(ends[grp - 1:grp, :], (SUBLANES, D_RNN))
    for j in range(4):
        gated = (hh[j] + pp[j] * h_prev) * _gelu_tanh(yg[j * grp:(j + 1) * grp, :])
        for c in range(N_SLABS):
            rn_s[c, pl.ds(j, grp, stride=4), :] = gated[:, c * LANES:(c + 1) * LANES]
    for c in range(N_SLABS):
        rnn_ref[:, c * LANES:(c + 1) * LANES] = rn_s[c].astype(BF16)

    cos = cos_ref[...]
    sin = sin_ref[...]
    for part in range(3):
        z = _dot(hn_s[...], w_ref[:, part * D_ATT:(part + 1) * D_ATT])
        for c in range(N_SLABS):
            zc = z[:, c * LANES:(c + 1) * LANES]
            if part < 2:
                zc = _rope_slab(zc, cos, sin)
            if part == 0:
                zc = zc * (1.0 / math.sqrt(HEAD_DIM))
            n = QKV_SLOT[part] * N_SLABS + c
            cols = slice(n * LANES, (n + 1) * LANES)
            cls_s[n] = zc
            for j in range(4):
                rows = zc[j * grp:(j + 1) * grp, :]
                qkv4_ref[0, j, :, cols] = rows.astype(BF16)
                nat_s[n, pl.ds(j, grp, stride=4), :] = rows
            for j in range(4):
                for jj in range(4):
                    rows = cls_s[n, pl.ds(j * grp + jj, n16, stride=4), :]
                    qkv16_ref[0, j + 4 * jj, :, cols] = rows.astype(BF16)
            nat = nat_s[n]
            qkv1_ref[:, cols] = nat.astype(BF16)
            if part == 1:
                klast_ref[:, c * LANES:(c + 1) * LANES] = nat
            if part == 2:
                vlast_ref[:, c * LANES:(c + 1) * LANES] = nat


def _class_major_positions(s, tm):
    r = np.arange(s)
    tile, within = r // tm, r % tm
    grp = tm // 4
    return (tile * tm + 4 * (within % grp) + within // grp).astype(np.int32)


def _front_call(x, g1, w_in, rope, cw, cb, wg, bga, bgx, lam, *, layer):
    s = x.shape[0]
    tm = TM_FRONT
    nt = s // tm
    n_last = MAX_WINDOW // tm
    row = lambda i: (i, 0)
    const2 = lambda i: (0, 0)
    const3 = lambda i: (0, 0, 0)
    last = lambda i: (jnp.maximum(i - (nt - n_last), 0), 0)
    t4 = 4 * BAND // tm
    t16 = ATT_BLOCK // tm
    out_shape = (
        jax.ShapeDtypeStruct((s, 3 * D_ATT), BF16),
        jax.ShapeDtypeStruct((s // (4 * BAND), 4, BAND, 3 * D_ATT), BF16),
        jax.ShapeDtypeStruct((s // ATT_BLOCK, ATT_STEPS, BAND, 3 * D_ATT), BF16),
        jax.ShapeDtypeStruct((s, D_RNN), BF16),
        jax.ShapeDtypeStruct((MAX_WINDOW, D_ATT), F32),
        jax.ShapeDtypeStruct((MAX_WINDOW, D_ATT), F32),
        jax.ShapeDtypeStruct((CONV_WIDTH - 1, SUBLANES, D_RNN), F32),
        jax.ShapeDtypeStruct((SUBLANES, D_RNN), F32),
    )
    out_specs = (
        pl.BlockSpec((tm, 3 * D_ATT), row),
        pl.BlockSpec((1, 4, tm // 4, 3 * D_ATT), lambda i: (i // t4, 0, i % t4, 0)),
        pl.BlockSpec((1, ATT_STEPS, tm // ATT_STEPS, 3 * D_ATT), lambda i: (i // t16, 0, i % t16, 0)),
        pl.BlockSpec((tm, D_RNN), row),
        pl.BlockSpec((tm, D_ATT), last),
        pl.BlockSpec((tm, D_ATT), last),
        pl.BlockSpec((CONV_WIDTH - 1, SUBLANES, D_RNN), const3),
        pl.BlockSpec((SUBLANES, D_RNN), const2),
    )
    in_specs = [
        pl.BlockSpec((tm, D_MODEL), row),
        pl.BlockSpec((1, D_MODEL), const2),
        pl.BlockSpec((None, D_MODEL, D_IN), lambda i: (layer, 0, 0), pipeline_mode=pl.Buffered(1)),
        pl.BlockSpec((tm, LANES), lambda i: (i, 0)),
        pl.BlockSpec((tm, LANES), lambda i: (i, 1)),
        pl.BlockSpec((CONV_WIDTH, D_RNN), const2),
        pl.BlockSpec((1, D_RNN), const2),
        pl.BlockSpec((2, D_RNN // 2, D_RNN), const3),
        pl.BlockSpec((1, D_RNN), const2),
        pl.BlockSpec((1, D_RNN), const2),
        pl.BlockSpec((1, D_RNN), const2),
    ]
    scratch = [
        pltpu.VMEM((D_MODEL // LANES, tm, LANES), F32),
        pltpu.VMEM((tm, D_MODEL), BF16),
        pltpu.VMEM((3 * N_SLABS, tm, LANES), F32),
        pltpu.VMEM((3 * N_SLABS, tm, LANES), F32),
        pltpu.VMEM((N_SLABS, tm, LANES), F32),
        pltpu.VMEM((CONV_WIDTH - 1, SUBLANES, D_RNN), F32),
        pltpu.VMEM((1, D_RNN), F32),
    ]
    return pl.pallas_call(
        functools.partial(_front_kernel, tm=tm),
        out_shape=out_shape,
        grid=(nt,),
        in_specs=in_specs,
        out_specs=out_specs,
        scratch_shapes=scratch,
        compiler_params=pltpu.CompilerParams(
            dimension_semantics=("arbitrary",), vmem_limit_bytes=VMEM_LIMIT),
        name="prompt_front",
    )(x, g1, w_in, rope, rope, cw, cb, wg, bga, bgx, lam)


def _band_bias():
    a = np.arange(BAND)[:, None]
    c = np.arange(2 * BAND)[None, :]
    dist = BAND + a - c
    ok = (dist >= 0) & (dist <= BAND)
    with_prev = np.where(ok, 0.0, NEG_BIG)
    no_prev = np.where(ok & (c >= BAND), 0.0, NEG_BIG)
    return np.tile(np.stack([with_prev, no_prev]), (1, 2, 1)).astype(np.float32)


def _merge_rows(hp, rows, o_new, m_new, l_new, acc_s, m_s, l_s):
    m_old = m_s[hp, rows, :]
    m_tot = jnp.maximum(m_old, m_new)
    w_old = jnp.exp(m_old - m_tot)
    w_new = jnp.exp(m_new - m_tot)
    m_s[hp, rows, :] = m_tot
    l_s[hp, rows, :] = l_s[hp, rows, :] * w_old + l_new * w_new
    acc_s[hp, rows, :] = acc_s[hp, rows, :] * w_old + o_new * w_new


def _attn_kernel(bias_ref, *refs, nt, aliased):
    pat_refs = [refs[3 * g:3 * g + 3] for g in range(3)]
    kc_ref, vc_ref = refs[9:11]
    refs = refs[(13 if aliased else 11):]
    o_ref, ko_ref, vo_ref = refs[0:3]
    acc_s, m_s, l_s, tmp_s, nat_s = refs[3:8]
    blk = pl.program_id(0)
    step = pl.program_id(1)
    quarter = ATT_BLOCK // 4
    piece = BAND // 4

    @pl.when(step == 0)
    def _():
        acc_s[...] = jnp.zeros_like(acc_s)
        l_s[...] = jnp.zeros_like(l_s)
        m_s[...] = jnp.full(m_s.shape, NEG_BIG, F32)

    w_buf = kc_ref.shape[-1]
    for h in range(kc_ref.shape[0]):
        ko_ref[h] = pltpu.roll(kc_ref[h], w_buf - nt, 1)
        vo_ref[h] = pltpu.roll(vc_ref[h], w_buf - nt, 1)

    lane = lax.broadcasted_iota(jnp.int32, (BAND, LANES), 1)
    low = lane < HEAD_DIM

    for sub in range(ATT_SUB):
        s = step * ATT_SUB + sub
        first = (blk * ATT_STEPS + s == 0, blk * 4 + s // 4 == 0, blk == 0)
        base4 = (s % 4) * quarter
        rows = (None,
                pl.ds(pl.multiple_of(base4 + (s // 4) * BAND, BAND), BAND),
                pl.ds(base4 + s // 4, BAND, stride=4))
        for g in range(3):
            q_ref, prev_ref, cur_ref = pat_refs[g]
            bias = bias_ref[jnp.where(first[g], 1, 0)]
            for hp in range(N_SLABS):
                kcols = slice(hp * LANES, (hp + 1) * LANES)
                vcols = slice(D_ATT + hp * LANES, D_ATT + (hp + 1) * LANES)
                if g == 0:
                    here = slice(sub * BAND, (sub + 1) * BAND)
                    before = slice((sub - 1) * BAND, sub * BAND)
                    q2 = q_ref[here, kcols]
                    kp, vp = ((prev_ref[:, kcols], prev_ref[:, vcols]) if sub == 0 else
                              (cur_ref[before, kcols], cur_ref[before, vcols]))
                    kc, vc = cur_ref[here, kcols], cur_ref[here, vcols]
                else:
                    q2 = q_ref[sub, :, kcols]
                    kp, vp = prev_ref[sub, :, kcols], prev_ref[sub, :, vcols]
                    kc, vc = cur_ref[sub, :, kcols], cur_ref[sub, :, vcols]
                k2 = jnp.concatenate([kp, kc], axis=0)
                v2 = jnp.concatenate([vp, vc], axis=0)
                zero = jnp.zeros_like(q2)
                qq = jnp.concatenate([jnp.where(low, q2, zero), jnp.where(low, zero, q2)], axis=0)
                sc = _dot_nt(qq, k2) + bias
                mx = jnp.max(sc, axis=1, keepdims=True)
                p = jnp.exp(sc - mx)
                sm = jnp.sum(p, axis=1, keepdims=True)
                oo = _dot(p.astype(BF16), v2)
                o_new = jnp.where(low, oo[:BAND], oo[BAND:])
                m_new = jnp.where(low, mx[:BAND], mx[BAND:])
                l_new = jnp.where(low, sm[:BAND], sm[BAND:])
                if g == 0:
                    slot = 3 * (sub * N_SLABS + hp)
                    for n, val in enumerate((o_new, m_new, l_new)):
                        tmp_s[slot + n] = val
                    for c4 in range(4):
                        pick = pl.ds(c4, piece, stride=4)
                        dst = pl.ds(pl.multiple_of(c4 * quarter + s * piece, piece), piece)
                        _merge_rows(hp, dst, tmp_s[slot, pick, :], tmp_s[slot + 1, pick, :],
                                    tmp_s[slot + 2, pick, :], acc_s, m_s, l_s)
                else:
                    _merge_rows(hp, rows[g], o_new, m_new, l_new, acc_s, m_s, l_s)

    @pl.when(step == pl.num_programs(1) - 1)
    def _():
        for hp in range(N_SLABS):
            for c4 in range(4):
                part = slice(c4 * quarter, (c4 + 1) * quarter)
                nat_s[pl.ds(c4, quarter, stride=4), :] = acc_s[hp, part, :] / l_s[hp, part, :]
            o_ref[:, hp * LANES:(hp + 1) * LANES] = nat_s[...].astype(BF16)


def _attn_call(qkv1, qkv4, qkv16, kc, vc, k_all, v_all, *, layer, nt):
    s = qkv1.shape[0]
    nb = s // ATT_BLOCK
    steps = ATT_STEPS // ATT_SUB
    shape5 = kc.shape
    depth, n_batch, n_heads, _, w_buf = shape5
    units = n_batch * n_heads
    hb = units // (nb * steps)
    assert hb * nb * steps == units
    flat = (depth, units, HEAD_DIM, w_buf)
    kc, vc = kc.reshape(flat), vc.reshape(flat)
    blk_c = pl.BlockSpec((None, hb, HEAD_DIM, w_buf), lambda b, t: (layer, b * steps + t, 0, 0))
    per4 = 4 // ATT_SUB
    wide = 3 * D_ATT
    q1 = qkv1.reshape(s // BAND, BAND, wide)
    q1_sub = qkv1.reshape(s // (ATT_SUB * BAND), ATT_SUB * BAND, wide)

    def width(is_q):
        return D_ATT if is_q else 2 * D_ATT

    def col(is_q):
        return QKV_SLOT[0] if is_q else 0

    def spec1(is_q, prev):
        if prev:
            return pl.BlockSpec(
                (None, BAND, width(is_q)),
                lambda b, t: (jnp.maximum((b * steps + t) * ATT_SUB - 1, 0), 0, col(is_q)))
        return pl.BlockSpec((None, ATT_SUB * BAND, width(is_q)),
                            lambda b, t: (b * steps + t, 0, col(is_q)))

    def spec4(is_q, prev):
        def im(b, t):
            j = b * 4 + t // per4
            return (jnp.maximum(j - 1, 0) if prev else j, t % per4, 0, col(is_q))
        return pl.BlockSpec((None, ATT_SUB, BAND, width(is_q)), im)

    def spec16(is_q, prev):
        def im(b, t):
            return (jnp.maximum(b - 1, 0) if prev else b, t, 0, col(is_q))
        return pl.BlockSpec((None, ATT_SUB, BAND, width(is_q)), im)

    in_specs = [pl.BlockSpec((2, 2 * BAND, 2 * BAND), lambda b, t: (0, 0, 0))]
    args = [jnp.asarray(_band_bias())]
    in_specs += [spec1(True, False), spec1(False, True), spec1(False, False)]
    args += [q1_sub, q1, q1_sub]
    for arr, mk in ((qkv4, spec4), (qkv16, spec16)):
        in_specs += [mk(True, False), mk(False, True), mk(False, False)]
        args += [arr] * 3
    in_specs += [blk_c, blk_c]
    args += [kc, vc]
    aliases = {}
    aliased = k_all is not None
    if aliased:
        aliases = {len(args): 1, len(args) + 1: 2}
        in_specs += [pl.BlockSpec(memory_space=pl.ANY), pl.BlockSpec(memory_space=pl.ANY)]
        args += [k_all.reshape(flat), v_all.reshape(flat)]
    full = jax.ShapeDtypeStruct(flat, F32)
    state = pltpu.VMEM((N_SLABS, ATT_BLOCK, LANES), F32)
    att, k_all, v_all = pl.pallas_call(
        functools.partial(_attn_kernel, nt=nt, aliased=aliased),
        out_shape=(jax.ShapeDtypeStruct((s, D_ATT), BF16), full, full),
        grid=(nb, steps),
        in_specs=in_specs,
        out_specs=(pl.BlockSpec((ATT_BLOCK, D_ATT), lambda b, t: (b, 0)), blk_c, blk_c),
        input_output_aliases=aliases,
        scratch_shapes=[state, state, state,
                        pltpu.VMEM((3 * ATT_SUB * N_SLABS, BAND, LANES), F32),
                        pltpu.VMEM((ATT_BLOCK, LANES), F32)],
        compiler_params=pltpu.CompilerParams(
            dimension_semantics=("arbitrary", "arbitrary"), vmem_limit_bytes=VMEM_LIMIT),
        name="prompt_attention",
    )(*args)
    return att, k_all.reshape(shape5), v_all.reshape(shape5)


def _ffn_kernel(x_ref, att_ref, rnn_ref, wo_ref, g2_ref, wg_ref, wu_ref, wd_ref, gf_ref,
                o_ref, act_s, *, final_norm):
    y = (x_ref[...] + _dot(att_ref[...], wo_ref[0:D_ATT, :])
         + _dot(rnn_ref[...], wo_ref[D_ATT:D_MODEL, :]))
    hf = _rmsnorm(y, g2_ref[...]).astype(BF16)
    for c in range(D_FF // FF_CHUNK):
        cols = slice(c * FF_CHUNK, (c + 1) * FF_CHUNK)
        gate = _dot(hf, wg_ref[:, cols])
        up = _dot(hf, wu_ref[:, cols])
        act_s[:, cols] = (gate * jax.nn.sigmoid(gate) * up).astype(BF16)
    out = y + _dot(act_s[...], wd_ref[...])
    if final_norm:
        out = _rmsnorm(out, gf_ref[...])
    o_ref[...] = out


def _ffn_call(x, att, rnn, wo, g2, wg, wu, wd, gf, *, layer, final_norm, tm):
    s = x.shape[0]
    row = lambda i: (i, 0)
    const = lambda i: (0, 0)
    pick = lambda i: (layer, 0, 0)
    once = pl.Buffered(1)
    in_specs = [
        pl.BlockSpec((tm, D_MODEL), row),
        pl.BlockSpec((tm, D_ATT), row),
        pl.BlockSpec((tm, D_RNN), row),
        pl.BlockSpec((None, D_MODEL, D_MODEL), pick, pipeline_mode=once),
        pl.BlockSpec((1, D_MODEL), const),
        pl.BlockSpec((None, D_MODEL, D_FF), pick, pipeline_mode=once),
        pl.BlockSpec((None, D_MODEL, D_FF), pick, pipeline_mode=once),
        pl.BlockSpec((None, D_FF, D_MODEL), pick, pipeline_mode=once),
        pl.BlockSpec((1, D_MODEL), const),
    ]
    return pl.pallas_call(
        functools.partial(_ffn_kernel, final_norm=final_norm),
        out_shape=jax.ShapeDtypeStruct((s, D_MODEL), F32),
        grid=(s // tm,),
        in_specs=in_specs,
        out_specs=pl.BlockSpec((tm, D_MODEL), row),
        scratch_shapes=[pltpu.VMEM((tm, D_FF), BF16)],
        compiler_params=pltpu.CompilerParams(
            dimension_semantics=("arbitrary",), vmem_limit_bytes=VMEM_LIMIT),
        name="out_proj_ffn",
    )(x, att, rnn, wo, g2, wg, wu, wd, gf)


def _sample_front_kernel(x_ref, g1_ref, w_ref, cos_ref, sin_ref, cw_ref, cb_ref, wg_ref,
                         bga_ref, bgx_ref, lam_ref, cst_ref, h0_ref,
                         q_ref, k_ref, v_ref, rnn_ref, nconv_ref, hl_ref, *, nb, nt):
    hn = _rmsnorm(x_ref[...], g1_ref[...]).astype(BF16)
    cos = cos_ref[...]
    sin = sin_ref[...]
    for part, out in enumerate((q_ref, k_ref, v_ref)):
        z = _dot(hn, w_ref[:, part * D_ATT:(part + 1) * D_ATT])
        for c in range(N_SLABS):
            zc = z[:, c * LANES:(c + 1) * LANES]
            if part < 2:
                zc = _rope_slab(zc, cos, sin)
            if part == 0:
                zc = zc * (1.0 / math.sqrt(HEAD_DIM))
            out[:, c * LANES:(c + 1) * LANES] = zc
    xr = _dot(hn, w_ref[:, 3 * D_ATT:3 * D_ATT + D_RNN])
    yg = _dot(hn, w_ref[:, 3 * D_ATT + D_RNN:])
    xpad = [cst_ref[j] for j in range(CONV_WIDTH - 1)]
    xpad += [xr[t * nb:(t + 1) * nb, :] for t in range(nt)]
    xc = []
    for t in range(nt):
        acc = cb_ref[...] + xpad[t] * cw_ref[0:1, :]
        for j in range(1, CONV_WIDTH):
            acc = acc + xpad[t + j] * cw_ref[j:j + 1, :]
        xc.append(acc)
    for j in range(CONV_WIDTH - 1):
        nconv_ref[j] = xpad[nt + j]
    a, b = _lru_coeffs(jnp.concatenate(xc, axis=0), wg_ref, bga_ref[...], bgx_ref[...],
                       lam_ref[...])
    h = h0_ref[...]
    hs = []
    for t in range(nt):
        h = a[t * nb:(t + 1) * nb, :] * h + b[t * nb:(t + 1) * nb, :]
        hs.append(h)
    hl_ref[...] = h
    rnn_ref[...] = (jnp.concatenate(hs, axis=0) * _gelu_tanh(yg)).astype(BF16)


def _sample_front_call(x, g1, w_in, rope, cw, cb, wg, bga, bgx, lam, cst, h0, *, layer, nb, nt):
    m = x.shape[0]

    def whole(a):
        return pl.BlockSpec(a.shape, lambda i, nd=a.ndim: (0,) * nd)

    in_specs = [whole(x), whole(g1),
                pl.BlockSpec((None, D_MODEL, D_IN), lambda i: (layer, 0, 0)),
                pl.BlockSpec((m, LANES), lambda i: (0, 0)), pl.BlockSpec((m, LANES), lambda i: (0, 1))]
    in_specs += [whole(a) for a in (cw, cb, wg, bga, bgx, lam, cst, h0)]
    out_shape = (
        jax.ShapeDtypeStruct((m, D_ATT), F32),
        jax.ShapeDtypeStruct((m, D_ATT), F32),
        jax.ShapeDtypeStruct((m, D_ATT), F32),
        jax.ShapeDtypeStruct((m, D_RNN), BF16),
        jax.ShapeDtypeStruct((CONV_WIDTH - 1, nb, D_RNN), F32),
        jax.ShapeDtypeStruct((nb, D_RNN), F32),
    )
    out_specs = tuple(pl.BlockSpec(o.shape, lambda i, nd=len(o.shape): (0,) * nd) for o in out_shape)
    return pl.pallas_call(
        functools.partial(_sample_front_kernel, nb=nb, nt=nt),
        out_shape=out_shape,
        grid=(1,),
        in_specs=in_specs,
        out_specs=out_specs,
        compiler_params=pltpu.CompilerParams(
            dimension_semantics=("arbitrary",), vmem_limit_bytes=VMEM_LIMIT),
        name="sample_front",
    )(x, g1, w_in, rope, rope, cw, cb, wg, bga, bgx, lam, cst, h0)


def _sample_mult(nt, w_buf):
    cache = np.zeros((SUBLANES, w_buf), np.float32)
    new = np.zeros((SUBLANES, LANES), np.float32)
    for dil in DILATIONS:
        for m in range(BAND + 1):
            idx = w_buf + np.arange(nt) - dil * m
            for t in range(nt):
                if idx[t] < 0:
                    continue
                if idx[t] < w_buf:
                    cache[t, idx[t]] += 1.0
                else:
                    new[t, LANES - nt + (idx[t] - w_buf)] += 1.0
    cache[nt:] = 1.0
    return cache, new


def _sample_attn_kernel(q_ref, kc_ref, vc_ref, kn_ref, vn_ref, mc_ref, mn_ref, k_any, v_any,
                        ko_ref, vo_ref, att_ref, *, nt):
    del k_any, v_any
    w_buf = kc_ref.shape[-1]
    mc = mc_ref[...]
    mn = mn_ref[...]
    lane = lax.broadcasted_iota(jnp.int32, (HEAD_DIM, LANES), 1)
    fresh = lane >= LANES - nt

    def tail_tiles(new_ref, pair):
        rows = jnp.concatenate([new_ref[0, :, pair * LANES:(pair + 1) * LANES],
                                jnp.zeros((LANES - SUBLANES, LANES), F32)], axis=0)
        placed = pltpu.roll(rows.T, LANES - nt, 1)
        return placed[:HEAD_DIM], placed[HEAD_DIM:]

    kn_tiles, vn_tiles = [], []
    for pair in range(CACHE_HEADS // 2):
        kn_tiles += tail_tiles(kn_ref, pair)
        vn_tiles += tail_tiles(vn_ref, pair)
    for h in range(CACHE_HEADS):
        q = q_ref[0, h].astype(BF16)
        kc = kc_ref[0, 0, h]
        vc = vc_ref[0, 0, h]
        kn = kn_tiles[h]
        vn = vn_tiles[h]
        sc = jnp.where(mc > 0, _dot(q, kc.astype(BF16)), NEG_BIG)
        sn = jnp.where(mn > 0, _dot(q, kn.astype(BF16)), NEG_BIG)
        mx = jnp.maximum(jnp.max(sc, axis=1, keepdims=True), jnp.max(sn, axis=1, keepdims=True))
        pc = mc * jnp.exp(sc - mx)
        pn = mn * jnp.exp(sn - mx)
        den = jnp.sum(pc, axis=1, keepdims=True) + jnp.sum(pn, axis=1, keepdims=True)
        o = _dot_nt(pc.astype(BF16), vc.astype(BF16)) + _dot_nt(pn.astype(BF16), vn.astype(BF16))
        att_ref[0, h] = o / den
        for src, new, dst in ((kc, kn, ko_ref), (vc, vn, vo_ref)):
            moved = pltpu.roll(src[:, w_buf - 2 * LANES:], 2 * LANES - nt, 1)[:, LANES:]
            dst[0, 0, h] = jnp.where(fresh, new, moved)


def _sample_attn_call(q, kc, vc, kn, vn, k_all, v_all, *, layer, nt):
    _, nb, nh, _, w_buf = kc.shape
    hb = CACHE_HEADS
    mc, mn = _sample_mult(nt, w_buf)
    blk_c = pl.BlockSpec((1, 1, hb, HEAD_DIM, w_buf), lambda b, h: (layer, b, h, 0, 0))
    blk_n = pl.BlockSpec((1, SUBLANES, hb * HEAD_DIM), lambda b, h: (b, 0, h))
    blk_q = pl.BlockSpec((1, hb, SUBLANES, HEAD_DIM), lambda b, h: (b, h, 0, 0))
    blk_o = pl.BlockSpec((1, 1, hb, HEAD_DIM, LANES), lambda b, h: (layer, b, h, 0, w_buf // LANES - 1))
    const = lambda b, h: (0, 0)
    in_specs = [blk_q, blk_c, blk_c, blk_n, blk_n,
                pl.BlockSpec((SUBLANES, w_buf), const), pl.BlockSpec((SUBLANES, LANES), const),
                pl.BlockSpec(memory_space=pl.ANY), pl.BlockSpec(memory_space=pl.ANY)]
    args = [q, kc, vc, kn, vn, jnp.asarray(mc), jnp.asarray(mn), k_all, v_all]
    aliases = {7: 0, 8: 1}
    full = jax.ShapeDtypeStruct(kc.shape, F32)
    return pl.pallas_call(
        functools.partial(_sample_attn_kernel, nt=nt),
        out_shape=(full, full, jax.ShapeDtypeStruct((nb, nh, SUBLANES, HEAD_DIM), F32)),
        grid=(nb, nh // hb),
        in_specs=in_specs,
        out_specs=(blk_o, blk_o, blk_q),
        input_output_aliases=aliases,
        compiler_params=pltpu.CompilerParams(
            dimension_semantics=("arbitrary", "arbitrary"), vmem_limit_bytes=VMEM_LIMIT),
        name="sample_attention",
    )(*args)


def _rope_tables(pos):
    half = ROT_DIM // 2
    inv = ROPE_THETA ** (-np.arange(half, dtype=np.float64) * 2.0 / ROT_DIM)
    ang = np.asarray(pos, np.float64)[:, None] * inv[None, :]
    cos, sin = np.cos(ang), np.sin(ang)
    n = ang.shape[0]
    rest = HEAD_DIM - ROT_DIM
    c = np.concatenate([cos, cos, np.ones((n, rest))], axis=1)
    t = np.concatenate([-sin, sin, np.zeros((n, rest))], axis=1)
    rep = LANES // HEAD_DIM
    return jnp.asarray(np.concatenate([np.tile(c, (1, rep)), np.tile(t, (1, rep))], axis=1), F32)


def _gate_weights(w_a, w_x):
    def dense(w):
        return jax.scipy.linalg.block_diag(*[w[n] for n in range(N_RNN_BLOCKS)])
    da, dx = dense(w_a), dense(w_x)
    half = D_RNN // 2
    chunks = [jnp.concatenate([da[c * half:(c + 1) * half, c * half:(c + 1) * half],
                               dx[c * half:(c + 1) * half, c * half:(c + 1) * half]], axis=1)
              for c in range(2)]
    return jnp.stack(chunks).astype(BF16)


def kernel(x_prompt, x_sample, cache_k, cache_v, state_conv, state_h, norm1_g, w_in, conv_w, conv_b,
           w_gate_a, b_gate_a, w_gate_x, b_gate_x, lru_lambda, w_out, norm2_g, w_ffn_gate, w_ffn_up,
           w_ffn_down, final_norm_g):
    bp, s, _ = x_prompt.shape
    nb, nt, _ = x_sample.shape
    depth = norm1_g.shape[0]
    w_buf = cache_k.shape[2]
    assert bp == 1 and s % ATT_BLOCK == 0 and w_buf == MAX_WINDOW and nt * nb == BAND

    xp = x_prompt.reshape(s, D_MODEL)
    xs = x_sample.transpose(1, 0, 2).reshape(nt * nb, D_MODEL)
    rope_p = _rope_tables(_class_major_positions(s, TM_FRONT))
    rope_s = _rope_tables(PAST_LEN + np.repeat(np.arange(nt), nb))
    ck_t = cache_k.transpose(0, 1, 3, 4, 2)
    cv_t = cache_v.transpose(0, 1, 3, 4, 2)
    cst = state_conv.transpose(0, 2, 1, 3)
    gf = final_norm_g.reshape(1, D_MODEL)
    row = lambda v: v.reshape(1, -1)

    pk, pv, pc, ph, sc, sh = [], [], [], [], [], []
    k_all = v_all = None
    w_in_b = w_in.astype(BF16)
    wo_b = w_out.astype(BF16)
    wg_b = w_ffn_gate.astype(BF16)
    wu_b = w_ffn_up.astype(BF16)
    wd_b = w_ffn_down.astype(BF16)
    for l in range(depth):
        gates = _gate_weights(w_gate_a[l], w_gate_x[l])
        common = (conv_w[l], row(conv_b[l]), gates, row(b_gate_a[l]), row(b_gate_x[l]),
                  row(lru_lambda[l]))
        final = l == depth - 1

        qkv1, qkv4, qkv16, rnn, k_last, v_last, tail, h_last = _front_call(
            xp, row(norm1_g[l]), w_in_b, rope_p, *common, layer=l)
        att, k_all, v_all = _attn_call(qkv1, qkv4, qkv16, ck_t, cv_t, k_all, v_all, layer=l, nt=nt)
        xp = _ffn_call(xp, att, rnn, wo_b, row(norm2_g[l]), wg_b, wu_b, wd_b, gf,
                       layer=l, final_norm=final, tm=TM_FFN)
        pk.append(k_last.reshape(1, MAX_WINDOW, N_HEADS, HEAD_DIM))
        pv.append(v_last.reshape(1, MAX_WINDOW, N_HEADS, HEAD_DIM))
        pc.append(tail[:, SUBLANES - 1, :].reshape(1, CONV_WIDTH - 1, D_RNN))
        ph.append(h_last[0:1])

        q_s, k_s, v_s, rnn_s, nconv, hl = _sample_front_call(
            xs, row(norm1_g[l]), w_in_b, rope_s, *common, cst[l], state_h[l],
            layer=l, nb=nb, nt=nt)

        def split(v):
            return v.reshape(nt, nb, N_HEADS, HEAD_DIM)

        q_b = jnp.pad(split(q_s).transpose(1, 2, 0, 3), ((0, 0), (0, 0), (0, SUBLANES - nt), (0, 0)))

        def fresh(v):
            return jnp.pad(v.reshape(nt, nb, D_ATT).transpose(1, 0, 2),
                           ((0, 0), (0, SUBLANES - nt), (0, 0)))

        k_all, v_all, att_s = _sample_attn_call(
            q_b, ck_t, cv_t, fresh(k_s), fresh(v_s), k_all, v_all, layer=l, nt=nt)
        att_s = att_s[:, :, :nt].transpose(2, 0, 1, 3).reshape(nt * nb, D_ATT).astype(BF16)
        xs = _ffn_call(xs, att_s, rnn_s, wo_b, row(norm2_g[l]), wg_b, wu_b, wd_b, gf,
                       layer=l, final_norm=final, tm=nt * nb)
        sc.append(nconv.transpose(1, 0, 2))
        sh.append(hl)

    y_prompt = xp.reshape(1, s, D_MODEL)
    y_sample = xs.reshape(nt, nb, D_MODEL).transpose(1, 0, 2)
    sample_k = k_all.transpose(0, 1, 4, 2, 3)
    sample_v = v_all.transpose(0, 1, 4, 2, 3)
    return (y_prompt, y_sample, jnp.stack(pk), jnp.stack(pv), jnp.stack(pc), jnp.stack(ph),
            sample_k, sample_v, jnp.stack(sc), jnp.stack(sh))
```

```python
import functools
import math

import numpy as np
import jax
import jax.numpy as jnp
from jax import lax
from jax.experimental import pallas as pl
from jax.experimental.pallas import tpu as pltpu

F32 = jnp.float32
BF16 = jnp.bfloat16

D_MODEL = 1024
HEAD_DIM = 64
D_ATT = 512
N_HEADS = 8
D_RNN = 512
N_RNN_BLOCKS = 8
CONV_WIDTH = 4
LRU_C = 8.0
DILATIONS = (1, 4, 16)
BAND = 128
MAX_WINDOW = 2048
ROPE_THETA = 500000.0
ROT_DIM = 16
D_FF = 2816
D_IN = 3 * D_ATT + 2 * D_RNN
RMS_EPS = 1e-6
PAST_LEN = 16384

LANES = 128
SUBLANES = 8
N_SLABS = D_ATT // LANES
FF_CHUNK = 256
NEG_BIG = -1e30
VMEM_LIMIT = 56 * 1024 * 1024

TM_FRONT = 512
TM_FFN = 512
ATT_BLOCK = MAX_WINDOW
ATT_STEPS = ATT_BLOCK // BAND
ATT_SUB = 2
CACHE_HEADS = 8
QKV_SLOT = (2, 0, 1)


def _dot(a, b):
    return jnp.dot(a, b, preferred_element_type=F32)


def _dot_nt(a, b):
    return lax.dot_general(a, b, (((1,), (1,)), ((), ())), preferred_element_type=F32)


def _rmsnorm(x, g):
    return x * lax.rsqrt(jnp.mean(x * x, axis=-1, keepdims=True) + RMS_EPS) * g


def _softplus(x):
    return jnp.maximum(x, 0.0) + jnp.log1p(jnp.exp(-jnp.abs(x)))


def _gelu_tanh(x):
    return 0.5 * x * (1.0 + jnp.tanh(math.sqrt(2.0 / math.pi) * (x + 0.044715 * (x * x * x))))


def _rope_slab(x, c, t):
    half = ROT_DIM // 2
    lane = lax.broadcasted_iota(jnp.int32, x.shape, 1)
    partner = jnp.where((lane & half) == 0, pltpu.roll(x, LANES - half, 1), pltpu.roll(x, half, 1))
    return x * c + partner * t


def _lru_coeffs(xc, wg_ref, bga, bgx, lam):
    xb = xc.astype(BF16)
    half = D_RNN // 2
    g0 = _dot(xb[:, :half], wg_ref[0])
    g1 = _dot(xb[:, half:], wg_ref[1])
    r = jax.nn.sigmoid(jnp.concatenate([g0[:, :half], g1[:, :half]], axis=1) + bga)
    ig = jax.nn.sigmoid(jnp.concatenate([g0[:, half:], g1[:, half:]], axis=1) + bgx)
    log_a = (-LRU_C) * r * _softplus(-lam)
    a = jnp.exp(log_a)
    th = jnp.tanh(log_a)
    b = jnp.sqrt(-2.0 * th / (1.0 - th)) * (ig * xc)
    return a, b


def _shift_rows(x, k, fill):
    row = lax.broadcasted_iota(jnp.int32, x.shape, 0)
    return jnp.where(row >= k, pltpu.roll(x, k, 0), fill)


def _front_kernel(x_ref, g1_ref, w_ref, cos_ref, sin_ref, cw_ref, cb_ref, wg_ref,
                  bga_ref, bgx_ref, lam_ref,
                  *refs, tm, aliased):
    if aliased:
        refs = refs[2:]
    qkv1_ref, qkv4_ref, qkv16_ref, rnn_ref, klast_ref, vlast_ref, tail_ref, hlast_ref = refs[:8]
    xs_s, hn_s, cls_s, nat_s, rn_s, carry_s, h_s = refs[8:]
    i = pl.program_id(0)
    grp = tm // 4
    n16 = tm // 16

    @pl.when(i == 0)
    def _():
        carry_s[...] = jnp.zeros_like(carry_s)
        h_s[...] = jnp.zeros_like(h_s)

    hn32 = _rmsnorm(x_ref[...], g1_ref[...])
    n_x = D_MODEL // LANES
    for c in range(n_x):
        xs_s[c] = hn32[:, c * LANES:(c + 1) * LANES]
    for j in range(4):
        for c in range(n_x):
            hn_s[j * grp:(j + 1) * grp, c * LANES:(c + 1) * LANES] = (
                xs_s[c, pl.ds(j, grp, stride=4), :].astype(BF16))

    xr = _dot(hn_s[...], w_ref[:, 3 * D_ATT:3 * D_ATT + D_RNN])
    yg = _dot(hn_s[...], w_ref[:, 3 * D_ATT + D_RNN:])
    slab = [xr[j * grp:(j + 1) * grp, :] for j in range(4)]
    back = [None] + [_shift_rows(slab[j], 1, carry_s[j - 1, SUBLANES - 1:SUBLANES, :])
                     for j in range(1, 4)]
    for j in range(1, 4):
        carry_s[j - 1] = slab[j][grp - SUBLANES:, :]
        tail_ref[j - 1] = slab[j][grp - SUBLANES:, :]
    w = [cw_ref[k:k + 1, :] for k in range(CONV_WIDTH)]
    taps = ((slab[0], back[3], back[2], back[1]),
            (slab[1], slab[0], back[3], back[2]),
            (slab[2], slab[1], slab[0], back[3]),
            (slab[3], slab[2], slab[1], slab[0]))
    xc = jnp.concatenate(
        [cb_ref[...] + t[0] * w[3] + t[1] * w[2] + t[2] * w[1] + t[3] * w[0] for t in taps], axis=0)
    a, b = _lru_coeffs(xc, wg_ref, bga_ref[...], bgx_ref[...], lam_ref[...])

    pp = [a[0:grp, :]]
    hh = [b[0:grp, :]]
    for j in range(1, 4):
        aj = a[j * grp:(j + 1) * grp, :]
        hh.append(aj * hh[j - 1] + b[j * grp:(j + 1) * grp, :])
        pp.append(aj * pp[j - 1])
    pc, hc = pp[3], hh[3]
    row = lax.broadcasted_iota(jnp.int32, (grp, D_RNN), 0)
    k = 1
    while k < grp:
        keep = row >= k
        hc = jnp.where(keep, hc + pc * pltpu.roll(hc, k, 0), hc)
        pc = jnp.where(keep, pc * pltpu.roll(pc, k, 0), pc)
        k *= 2
    h_in = h_s[...]
    ends = hc + pc * h_in
    h_prev = _shift_rows(ends, 1, h_in)
    h_s[...] = ends[grp - 1:grp, :]
    hlast_ref[...] = jnp.broadcast_to(ends[grp - 1:grp, :], (SUBLANES, D_RNN))
    for j in range(4):
        gated = (hh[j] + pp[j] * h_prev) * _gelu_tanh(yg[j * grp:(j + 1) * grp, :])
        for c in range(N_SLABS):
            rn_s[c, pl.ds(j, grp, stride=4), :] = gated[:, c * LANES:(c + 1) * LANES]
    for c in range(N_SLABS):
        rnn_ref[:, c * LANES:(c + 1) * LANES] = rn_s[c].astype(BF16)

    cos = cos_ref[...]
    sin = sin_ref[...]
    for part in range(3):
        z = _dot(hn_s[...], w_ref[:, part * D_ATT:(part + 1) * D_ATT])
        for c in range(N_SLABS):
            zc = z[:, c * LANES:(c + 1) * LANES]
            if part < 2:
                zc = _rope_slab(zc, cos, sin)
            if part == 0:
                zc = zc * (1.0 / math.sqrt(HEAD_DIM))
            n = QKV_SLOT[part] * N_SLABS + c
            cols = slice(n * LANES, (n + 1) * LANES)
            cls_s[n] = zc
            for j in range(4):
                rows = zc[j * grp:(j + 1) * grp, :]
                qkv4_ref[0, j, :, cols] = rows.astype(BF16)
                nat_s[n, pl.ds(j, grp, stride=4), :] = rows
            for j in range(4):
                for jj in range(4):
                    rows = cls_s[n, pl.ds(j * grp + jj, n16, stride=4), :]
                    qkv16_ref[0, j + 4 * jj, :, cols] = rows.astype(BF16)
            qkv1_ref[:, cols] = nat_s[n].astype(BF16)

    @pl.when(i >= pl.num_programs(0) - MAX_WINDOW // tm)
    def _():
        for c in range(N_SLABS):
            rows = slice(c * LANES, (c + 1) * LANES)
            klast_ref[rows, :] = nat_s[QKV_SLOT[1] * N_SLABS + c].T
            vlast_ref[rows, :] = nat_s[QKV_SLOT[2] * N_SLABS + c].T


def _class_major_positions(s, tm):
    r = np.arange(s)
    tile, within = r // tm, r % tm
    grp = tm // 4
    return (tile * tm + 4 * (within % grp) + within // grp).astype(np.int32)


def _front_call(x, g1, w_in, rope, cw, cb, wg, bga, bgx, lam, k_last, v_last, *, layer):
    s = x.shape[0]
    depth = w_in.shape[0]
    tm = TM_FRONT
    nt = s // tm
    n_last = MAX_WINDOW // tm
    row = lambda i: (i, 0)
    const2 = lambda i: (0, 0)
    const3 = lambda i: (0, 0, 0)
    last = lambda i: (layer, 0, jnp.maximum(i - (nt - n_last), 0))
    t4 = 4 * BAND // tm
    t16 = ATT_BLOCK // tm
    out_shape = (
        jax.ShapeDtypeStruct((s, 3 * D_ATT), BF16),
        jax.ShapeDtypeStruct((s // (4 * BAND), 4, BAND, 3 * D_ATT), BF16),
        jax.ShapeDtypeStruct((s // ATT_BLOCK, ATT_STEPS, BAND, 3 * D_ATT), BF16),
        jax.ShapeDtypeStruct((s, D_RNN), BF16),
        jax.ShapeDtypeStruct((depth, D_ATT, MAX_WINDOW), F32),
        jax.ShapeDtypeStruct((depth, D_ATT, MAX_WINDOW), F32),
        jax.ShapeDtypeStruct((CONV_WIDTH - 1, SUBLANES, D_RNN), F32),
        jax.ShapeDtypeStruct((SUBLANES, D_RNN), F32),
    )
    out_specs = (
        pl.BlockSpec((tm, 3 * D_ATT), row),
        pl.BlockSpec((1, 4, tm // 4, 3 * D_ATT), lambda i: (i // t4, 0, i % t4, 0)),
        pl.BlockSpec((1, ATT_STEPS, tm // ATT_STEPS, 3 * D_ATT), lambda i: (i // t16, 0, i % t16, 0)),
        pl.BlockSpec((tm, D_RNN), row),
        pl.BlockSpec((None, D_ATT, tm), last),
        pl.BlockSpec((None, D_ATT, tm), last),
        pl.BlockSpec((CONV_WIDTH - 1, SUBLANES, D_RNN), const3),
        pl.BlockSpec((SUBLANES, D_RNN), const2),
    )
    in_specs = [
        pl.BlockSpec((tm, D_MODEL), row),
        pl.BlockSpec((1, D_MODEL), const2),
        pl.BlockSpec((None, D_MODEL, D_IN), lambda i: (layer, 0, 0), pipeline_mode=pl.Buffered(1)),
        pl.BlockSpec((tm, LANES), lambda i: (i, 0)),
        pl.BlockSpec((tm, LANES), lambda i: (i, 1)),
        pl.BlockSpec((CONV_WIDTH, D_RNN), const2),
        pl.BlockSpec((1, D_RNN), const2),
        pl.BlockSpec((2, D_RNN // 2, D_RNN), const3),
        pl.BlockSpec((1, D_RNN), const2),
        pl.BlockSpec((1, D_RNN), const2),
        pl.BlockSpec((1, D_RNN), const2),
    ]
    scratch = [
        pltpu.VMEM((D_MODEL // LANES, tm, LANES), F32),
        pltpu.VMEM((tm, D_MODEL), BF16),
        pltpu.VMEM((3 * N_SLABS, tm, LANES), F32),
        pltpu.VMEM((3 * N_SLABS, tm, LANES), F32),
        pltpu.VMEM((N_SLABS, tm, LANES), F32),
        pltpu.VMEM((CONV_WIDTH - 1, SUBLANES, D_RNN), F32),
        pltpu.VMEM((1, D_RNN), F32),
    ]
    args = [x, g1, w_in, rope, rope, cw, cb, wg, bga, bgx, lam]
    aliases = {}
    aliased = k_last is not None
    if aliased:
        aliases = {len(args): 4, len(args) + 1: 5}
        in_specs += [pl.BlockSpec(memory_space=pl.ANY), pl.BlockSpec(memory_space=pl.ANY)]
        args += [k_last, v_last]
    return pl.pallas_call(
        functools.partial(_front_kernel, tm=tm, aliased=aliased),
        out_shape=out_shape,
        grid=(nt,),
        in_specs=in_specs,
        out_specs=out_specs,
        scratch_shapes=scratch,
        input_output_aliases=aliases,
        compiler_params=pltpu.CompilerParams(
            dimension_semantics=("arbitrary",), vmem_limit_bytes=VMEM_LIMIT),
        name="prompt_front",
    )(*args)


def _band_bias():
    a = np.arange(BAND)[:, None]
    c = np.arange(2 * BAND)[None, :]
    dist = BAND + a - c
    ok = (dist >= 0) & (dist <= BAND)
    with_prev = np.where(ok, 0.0, NEG_BIG)
    no_prev = np.where(ok & (c >= BAND), 0.0, NEG_BIG)
    return np.tile(np.stack([with_prev, no_prev]), (1, 2, 1)).astype(np.float32)


def _merge_rows(hp, rows, o_new, m_new, l_new, acc_s, m_s, l_s):
    m_old = m_s[hp, rows, :]
    m_tot = jnp.maximum(m_old, m_new)
    w_old = jnp.exp(m_old - m_tot)
    w_new = jnp.exp(m_new - m_tot)
    m_s[hp, rows, :] = m_tot
    l_s[hp, rows, :] = l_s[hp, rows, :] * w_old + l_new * w_new
    acc_s[hp, rows, :] = acc_s[hp, rows, :] * w_old + o_new * w_new


def _attn_kernel(bias_ref, *refs, nt, aliased):
    pat_refs = [refs[3 * g:3 * g + 3] for g in range(3)]
    kc_ref, vc_ref = refs[9:11]
    refs = refs[(13 if aliased else 11):]
    o_ref, ko_ref, vo_ref = refs[0:3]
    acc_s, m_s, l_s, tmp_s, nat_s = refs[3:8]
    blk = pl.program_id(0)
    step = pl.program_id(1)
    quarter = ATT_BLOCK // 4
    piece = BAND // 4

    @pl.when(step == 0)
    def _():
        acc_s[...] = jnp.zeros_like(acc_s)
        l_s[...] = jnp.zeros_like(l_s)
        m_s[...] = jnp.full(m_s.shape, NEG_BIG, F32)

    w_buf = kc_ref.shape[-1]
    for h in range(kc_ref.shape[0]):
        ko_ref[h] = pltpu.roll(kc_ref[h], w_buf - nt, 1)
        vo_ref[h] = pltpu.roll(vc_ref[h], w_buf - nt, 1)

    lane = lax.broadcasted_iota(jnp.int32, (BAND, LANES), 1)
    low = lane < HEAD_DIM

    for sub in range(ATT_SUB):
        s = step * ATT_SUB + sub
        first = (blk * ATT_STEPS + s == 0, blk * 4 + s // 4 == 0, blk == 0)
        base4 = (s % 4) * quarter
        rows = (None,
                pl.ds(pl.multiple_of(base4 + (s // 4) * BAND, BAND), BAND),
                pl.ds(base4 + s // 4, BAND, stride=4))
        for g in range(3):
            q_ref, prev_ref, cur_ref = pat_refs[g]
            bias = bias_ref[jnp.where(first[g], 1, 0)]
            for hp in range(N_SLABS):
                kcols = slice(hp * LANES, (hp + 1) * LANES)
                vcols = slice(D_ATT + hp * LANES, D_ATT + (hp + 1) * LANES)
                if g == 0:
                    here = slice(sub * BAND, (sub + 1) * BAND)
                    before = slice((sub - 1) * BAND, sub * BAND)
                    q2 = q_ref[here, kcols]
                    kp, vp = ((prev_ref[:, kcols], prev_ref[:, vcols]) if sub == 0 else
                              (cur_ref[before, kcols], cur_ref[before, vcols]))
                    kc, vc = cur_ref[here, kcols], cur_ref[here, vcols]
                else:
                    q2 = q_ref[sub, :, kcols]
                    kp, vp = prev_ref[sub, :, kcols], prev_ref[sub, :, vcols]
                    kc, vc = cur_ref[sub, :, kcols], cur_ref[sub, :, vcols]
                k2 = jnp.concatenate([kp, kc], axis=0)
                v2 = jnp.concatenate([vp, vc], axis=0)
                zero = jnp.zeros_like(q2)
                qq = jnp.concatenate([jnp.where(low, q2, zero), jnp.where(low, zero, q2)], axis=0)
                sc = _dot_nt(qq, k2) + bias
                mx = jnp.max(sc, axis=1, keepdims=True)
                p = jnp.exp(sc - mx)
                sm = jnp.sum(p, axis=1, keepdims=True)
                oo = _dot(p.astype(BF16), v2)
                o_new = jnp.where(low, oo[:BAND], oo[BAND:])
                m_new = jnp.where(low, mx[:BAND], mx[BAND:])
                l_new = jnp.where(low, sm[:BAND], sm[BAND:])
                if g == 0:
                    slot = 3 * (sub * N_SLABS + hp)
                    for n, val in enumerate((o_new, m_new, l_new)):
                        tmp_s[slot + n] = val
                    for c4 in range(4):
                        pick = pl.ds(c4, piece, stride=4)
                        dst = pl.ds(pl.multiple_of(c4 * quarter + s * piece, piece), piece)
                        _merge_rows(hp, dst, tmp_s[slot, pick, :], tmp_s[slot + 1, pick, :],
                                    tmp_s[slot + 2, pick, :], acc_s, m_s, l_s)
                else:
                    _merge_rows(hp, rows[g], o_new, m_new, l_new, acc_s, m_s, l_s)

    @pl.when(step == pl.num_programs(1) - 1)
    def _():
        for hp in range(N_SLABS):
            for c4 in range(4):
                part = slice(c4 * quarter, (c4 + 1) * quarter)
                nat_s[pl.ds(c4, quarter, stride=4), :] = acc_s[hp, part, :] / l_s[hp, part, :]
            o_ref[:, hp * LANES:(hp + 1) * LANES] = nat_s[...].astype(BF16)


def _attn_call(qkv1, qkv4, qkv16, kc, vc, k_all, v_all, *, layer, nt):
    s = qkv1.shape[0]
    nb = s // ATT_BLOCK
    steps = ATT_STEPS // ATT_SUB
    shape5 = kc.shape
    depth, n_batch, n_heads, _, w_buf = shape5
    units = n_batch * n_heads
    hb = units // (nb * steps)
    assert hb * nb * steps == units
    flat = (depth, units, HEAD_DIM, w_buf)
    kc, vc = kc.reshape(flat), vc.reshape(flat)
    blk_c = pl.BlockSpec((None, hb, HEAD_DIM, w_buf), lambda b, t: (layer, b * steps + t, 0, 0))
    per4 = 4 // ATT_SUB
    wide = 3 * D_ATT
    q1 = qkv1.reshape(s // BAND, BAND, wide)
    q1_sub = qkv1.reshape(s // (ATT_SUB * BAND), ATT_SUB * BAND, wide)

    def width(is_q):
        return D_ATT if is_q else 2 * D_ATT

    def col(is_q):
        return QKV_SLOT[0] if is_q else 0

    def spec1(is_q, prev):
        if prev:
            return pl.BlockSpec(
                (None, BAND, width(is_q)),
                lambda b, t: (jnp.maximum((b * steps + t) * ATT_SUB - 1, 0), 0, col(is_q)))
        return pl.BlockSpec((None, ATT_SUB * BAND, width(is_q)),
                            lambda b, t: (b * steps + t, 0, col(is_q)))

    def spec4(is_q, prev):
        def im(b, t):
            j = b * 4 + t // per4
            return (jnp.maximum(j - 1, 0) if prev else j, t % per4, 0, col(is_q))
        return pl.BlockSpec((None, ATT_SUB, BAND, width(is_q)), im)

    def spec16(is_q, prev):
        def im(b, t):
            return (jnp.maximum(b - 1, 0) if prev else b, t, 0, col(is_q))
        return pl.BlockSpec((None, ATT_SUB, BAND, width(is_q)), im)

    in_specs = [pl.BlockSpec((2, 2 * BAND, 2 * BAND), lambda b, t: (0, 0, 0))]
    args = [jnp.asarray(_band_bias())]
    in_specs += [spec1(True, False), spec1(False, True), spec1(False, False)]
    args += [q1_sub, q1, q1_sub]
    for arr, mk in ((qkv4, spec4), (qkv16, spec16)):
        in_specs += [mk(True, False), mk(False, True), mk(False, False)]
        args += [arr] * 3
    in_specs += [blk_c, blk_c]
    args += [kc, vc]
    aliases = {}
    aliased = k_all is not None
    if aliased:
        aliases = {len(args): 1, len(args) + 1: 2}
        in_specs += [pl.BlockSpec(memory_space=pl.ANY), pl.BlockSpec(memory_space=pl.ANY)]
        args += [k_all.reshape(flat), v_all.reshape(flat)]
    full = jax.ShapeDtypeStruct(flat, F32)
    state = pltpu.VMEM((N_SLABS, ATT_BLOCK, LANES), F32)
    att, k_all, v_all = pl.pallas_call(
        functools.partial(_attn_kernel, nt=nt, aliased=aliased),
        out_shape=(jax.ShapeDtypeStruct((s, D_ATT), BF16), full, full),
        grid=(nb, steps),
        in_specs=in_specs,
        out_specs=(pl.BlockSpec((ATT_BLOCK, D_ATT), lambda b, t: (b, 0)), blk_c, blk_c),
        input_output_aliases=aliases,
        scratch_shapes=[state, state, state,
                        pltpu.VMEM((3 * ATT_SUB * N_SLABS, BAND, LANES), F32),
                        pltpu.VMEM((ATT_BLOCK, LANES), F32)],
        compiler_params=pltpu.CompilerParams(
            dimension_semantics=("arbitrary", "arbitrary"), vmem_limit_bytes=VMEM_LIMIT),
        name="prompt_attention",
    )(*args)
    return att, k_all.reshape(shape5), v_all.reshape(shape5)


def _ffn_kernel(x_ref, att_ref, rnn_ref, wo_ref, g2_ref, wg_ref, wu_ref, wd_ref, gf_ref,
                o_ref, act_s, *, final_norm):
    y = (x_ref[...] + _dot(att_ref[...], wo_ref[0:D_ATT, :])
         + _dot(rnn_ref[...], wo_ref[D_ATT:D_MODEL, :]))
    hf = _rmsnorm(y, g2_ref[...]).astype(BF16)
    for c in range(D_FF // FF_CHUNK):
        cols = slice(c * FF_CHUNK, (c + 1) * FF_CHUNK)
        gate = _dot(hf, wg_ref[:, cols])
        up = _dot(hf, wu_ref[:, cols])
        act_s[:, cols] = (gate * jax.nn.sigmoid(gate) * up).astype(BF16)
    out = y + _dot(act_s[...], wd_ref[...])
    if final_norm:
        out = _rmsnorm(out, gf_ref[...])
    o_ref[...] = out


def _ffn_call(x, att, rnn, wo, g2, wg, wu, wd, gf, *, layer, final_norm, tm):
    s = x.shape[0]
    row = lambda i: (i, 0)
    const = lambda i: (0, 0)
    pick = lambda i: (layer, 0, 0)
    once = pl.Buffered(1)
    in_specs = [
        pl.BlockSpec((tm, D_MODEL), row),
        pl.BlockSpec((tm, D_ATT), row),
        pl.BlockSpec((tm, D_RNN), row),
        pl.BlockSpec((None, D_MODEL, D_MODEL), pick, pipeline_mode=once),
        pl.BlockSpec((1, D_MODEL), const),
        pl.BlockSpec((None, D_MODEL, D_FF), pick, pipeline_mode=once),
        pl.BlockSpec((None, D_MODEL, D_FF), pick, pipeline_mode=once),
        pl.BlockSpec((None, D_FF, D_MODEL), pick, pipeline_mode=once),
        pl.BlockSpec((1, D_MODEL), const),
    ]
    return pl.pallas_call(
        functools.partial(_ffn_kernel, final_norm=final_norm),
        out_shape=jax.ShapeDtypeStruct((s, D_MODEL), F32),
        grid=(s // tm,),
        in_specs=in_specs,
        out_specs=pl.BlockSpec((tm, D_MODEL), row),
        scratch_shapes=[pltpu.VMEM((tm, D_FF), BF16)],
        compiler_params=pltpu.CompilerParams(
            dimension_semantics=("arbitrary",), vmem_limit_bytes=VMEM_LIMIT),
        name="out_proj_ffn",
    )(x, att, rnn, wo, g2, wg, wu, wd, gf)


def _sample_front_kernel(x_ref, g1_ref, w_ref, cos_ref, sin_ref, cw_ref, cb_ref, wg_ref,
                         bga_ref, bgx_ref, lam_ref, cst_ref, h0_ref,
                         q_ref, k_ref, v_ref, rnn_ref, nconv_ref, hl_ref, *, nb, nt):
    hn = _rmsnorm(x_ref[...], g1_ref[...]).astype(BF16)
    cos = cos_ref[...]
    sin = sin_ref[...]
    for part, out in enumerate((q_ref, k_ref, v_ref)):
        z = _dot(hn, w_ref[:, part * D_ATT:(part + 1) * D_ATT])
        for c in range(N_SLABS):
            zc = z[:, c * LANES:(c + 1) * LANES]
            if part < 2:
                zc = _rope_slab(zc, cos, sin)
            if part == 0:
                zc = zc * (1.0 / math.sqrt(HEAD_DIM))
            out[:, c * LANES:(c + 1) * LANES] = zc
    xr = _dot(hn, w_ref[:, 3 * D_ATT:3 * D_ATT + D_RNN])
    yg = _dot(hn, w_ref[:, 3 * D_ATT + D_RNN:])
    xpad = [cst_ref[j] for j in range(CONV_WIDTH - 1)]
    xpad += [xr[t * nb:(t + 1) * nb, :] for t in range(nt)]
    xc = []
    for t in range(nt):
        acc = cb_ref[...] + xpad[t] * cw_ref[0:1, :]
        for j in range(1, CONV_WIDTH):
            acc = acc + xpad[t + j] * cw_ref[j:j + 1, :]
        xc.append(acc)
    for j in range(CONV_WIDTH - 1):
        nconv_ref[j] = xpad[nt + j]
    a, b = _lru_coeffs(jnp.concatenate(xc, axis=0), wg_ref, bga_ref[...], bgx_ref[...],
                       lam_ref[...])
    h = h0_ref[...]
    hs = []
    for t in range(nt):
        h = a[t * nb:(t + 1) * nb, :] * h + b[t * nb:(t + 1) * nb, :]
        hs.append(h)
    hl_ref[...] = h
    rnn_ref[...] = (jnp.concatenate(hs, axis=0) * _gelu_tanh(yg)).astype(BF16)


def _sample_front_call(x, g1, w_in, rope, cw, cb, wg, bga, bgx, lam, cst, h0, *, layer, nb, nt):
    m = x.shape[0]

    def whole(a):
        return pl.BlockSpec(a.shape, lambda i, nd=a.ndim: (0,) * nd)

    in_specs = [whole(x), whole(g1),
                pl.BlockSpec((None, D_MODEL, D_IN), lambda i: (layer, 0, 0)),
                pl.BlockSpec((m, LANES), lambda i: (0, 0)), pl.BlockSpec((m, LANES), lambda i: (0, 1))]
    in_specs += [whole(a) for a in (cw, cb, wg, bga, bgx, lam, cst, h0)]
    out_shape = (
        jax.ShapeDtypeStruct((m, D_ATT), F32),
        jax.ShapeDtypeStruct((m, D_ATT), F32),
        jax.ShapeDtypeStruct((m, D_ATT), F32),
        jax.ShapeDtypeStruct((m, D_RNN), BF16),
        jax.ShapeDtypeStruct((CONV_WIDTH - 1, nb, D_RNN), F32),
        jax.ShapeDtypeStruct((nb, D_RNN), F32),
    )
    out_specs = tuple(pl.BlockSpec(o.shape, lambda i, nd=len(o.shape): (0,) * nd) for o in out_shape)
    return pl.pallas_call(
        functools.partial(_sample_front_kernel, nb=nb, nt=nt),
        out_shape=out_shape,
        grid=(1,),
        in_specs=in_specs,
        out_specs=out_specs,
        compiler_params=pltpu.CompilerParams(
            dimension_semantics=("arbitrary",), vmem_limit_bytes=VMEM_LIMIT),
        name="sample_front",
    )(x, g1, w_in, rope, rope, cw, cb, wg, bga, bgx, lam, cst, h0)


def _sample_mult(nt, w_buf):
    cache = np.zeros((SUBLANES, w_buf), np.float32)
    new = np.zeros((SUBLANES, LANES), np.float32)
    for dil in DILATIONS:
        for m in range(BAND + 1):
            idx = w_buf + np.arange(nt) - dil * m
            for t in range(nt):
                if idx[t] < 0:
                    continue
                if idx[t] < w_buf:
                    cache[t, idx[t]] += 1.0
                else:
                    new[t, LANES - nt + (idx[t] - w_buf)] += 1.0
    cache[nt:] = 1.0
    return cache, new


def _sample_attn_kernel(q_ref, kc_ref, vc_ref, kn_ref, vn_ref, mc_ref, mn_ref, k_any, v_any,
                        ko_ref, vo_ref, att_ref, *, nt):
    del k_any, v_any
    w_buf = kc_ref.shape[-1]
    mc = mc_ref[...]
    mn = mn_ref[...]
    lane = lax.broadcasted_iota(jnp.int32, (HEAD_DIM, LANES), 1)
    fresh = lane >= LANES - nt

    def tail_tiles(new_ref, pair):
        rows = jnp.concatenate([new_ref[0, :, pair * LANES:(pair + 1) * LANES],
                                jnp.zeros((LANES - SUBLANES, LANES), F32)], axis=0)
        placed = pltpu.roll(rows.T, LANES - nt, 1)
        return placed[:HEAD_DIM], placed[HEAD_DIM:]

    kn_tiles, vn_tiles = [], []
    for pair in range(CACHE_HEADS // 2):
        kn_tiles += tail_tiles(kn_ref, pair)
        vn_tiles += tail_tiles(vn_ref, pair)
    for h in range(CACHE_HEADS):
        q = q_ref[0, h].astype(BF16)
        kc = kc_ref[0, 0, h]
        vc = vc_ref[0, 0, h]
        kn = kn_tiles[h]
        vn = vn_tiles[h]
        sc = jnp.where(mc > 0, _dot(q, kc.astype(BF16)), NEG_BIG)
        sn = jnp.where(mn > 0, _dot(q, kn.astype(BF16)), NEG_BIG)
        mx = jnp.maximum(jnp.max(sc, axis=1, keepdims=True), jnp.max(sn, axis=1, keepdims=True))
        pc = mc * jnp.exp(sc - mx)
        pn = mn * jnp.exp(sn - mx)
        den = jnp.sum(pc, axis=1, keepdims=True) + jnp.sum(pn, axis=1, keepdims=True)
        o = _dot_nt(pc.astype(BF16), vc.astype(BF16)) + _dot_nt(pn.astype(BF16), vn.astype(BF16))
        att_ref[0, h] = o / den
        for src, new, dst in ((kc, kn, ko_ref), (vc, vn, vo_ref)):
            moved = pltpu.roll(src[:, w_buf - 2 * LANES:], 2 * LANES - nt, 1)[:, LANES:]
            dst[0, 0, h] = jnp.where(fresh, new, moved)


def _sample_attn_call(q, kc, vc, kn, vn, k_all, v_all, *, layer, nt):
    _, nb, nh, _, w_buf = kc.shape
    hb = CACHE_HEADS
    mc, mn = _sample_mult(nt, w_buf)
    blk_c = pl.BlockSpec((1, 1, hb, HEAD_DIM, w_buf), lambda b, h: (layer, b, h, 0, 0))
    blk_n = pl.BlockSpec((1, SUBLANES, hb * HEAD_DIM), lambda b, h: (b, 0, h))
    blk_q = pl.BlockSpec((1, hb, SUBLANES, HEAD_DIM), lambda b, h: (b, h, 0, 0))
    blk_o = pl.BlockSpec((1, 1, hb, HEAD_DIM, LANES), lambda b, h: (layer, b, h, 0, w_buf // LANES - 1))
    const = lambda b, h: (0, 0)
    in_specs = [blk_q, blk_c, blk_c, blk_n, blk_n,
                pl.BlockSpec((SUBLANES, w_buf), const), pl.BlockSpec((SUBLANES, LANES), const),
                pl.BlockSpec(memory_space=pl.ANY), pl.BlockSpec(memory_space=pl.ANY)]
    args = [q, kc, vc, kn, vn, jnp.asarray(mc), jnp.asarray(mn), k_all, v_all]
    aliases = {7: 0, 8: 1}
    full = jax.ShapeDtypeStruct(kc.shape, F32)
    return pl.pallas_call(
        functools.partial(_sample_attn_kernel, nt=nt),
        out_shape=(full, full, jax.ShapeDtypeStruct((nb, nh, SUBLANES, HEAD_DIM), F32)),
        grid=(nb, nh // hb),
        in_specs=in_specs,
        out_specs=(blk_o, blk_o, blk_q),
        input_output_aliases=aliases,
        compiler_params=pltpu.CompilerParams(
            dimension_semantics=("arbitrary", "arbitrary"), vmem_limit_bytes=VMEM_LIMIT),
        name="sample_attention",
    )(*args)


def _rope_tables(pos):
    half = ROT_DIM // 2
    inv = ROPE_THETA ** (-np.arange(half, dtype=np.float64) * 2.0 / ROT_DIM)
    ang = np.asarray(pos, np.float64)[:, None] * inv[None, :]
    cos, sin = np.cos(ang), np.sin(ang)
    n = ang.shape[0]
    rest = HEAD_DIM - ROT_DIM
    c = np.concatenate([cos, cos, np.ones((n, rest))], axis=1)
    t = np.concatenate([-sin, sin, np.zeros((n, rest))], axis=1)
    rep = LANES // HEAD_DIM
    return jnp.asarray(np.concatenate([np.tile(c, (1, rep)), np.tile(t, (1, rep))], axis=1), F32)


def _gate_weights(w_a, w_x):
    def dense(w):
        return jax.scipy.linalg.block_diag(*[w[n] for n in range(N_RNN_BLOCKS)])
    da, dx = dense(w_a), dense(w_x)
    half = D_RNN // 2
    chunks = [jnp.concatenate([da[c * half:(c + 1) * half, c * half:(c + 1) * half],
                               dx[c * half:(c + 1) * half, c * half:(c + 1) * half]], axis=1)
              for c in range(2)]
    return jnp.stack(chunks).astype(BF16)


def kernel(x_prompt, x_sample, cache_k, cache_v, state_conv, state_h, norm1_g, w_in, conv_w, conv_b,
           w_gate_a, b_gate_a, w_gate_x, b_gate_x, lru_lambda, w_out, norm2_g, w_ffn_gate, w_ffn_up,
           w_ffn_down, final_norm_g):
    bp, s, _ = x_prompt.shape
    nb, nt, _ = x_sample.shape
    depth = norm1_g.shape[0]
    w_buf = cache_k.shape[2]
    assert bp == 1 and s % ATT_BLOCK == 0 and w_buf == MAX_WINDOW and nt * nb == BAND

    xp = x_prompt.reshape(s, D_MODEL)
    xs = x_sample.transpose(1, 0, 2).reshape(nt * nb, D_MODEL)
    rope_p = _rope_tables(_class_major_positions(s, TM_FRONT))
    rope_s = _rope_tables(PAST_LEN + np.repeat(np.arange(nt), nb))
    ck_t = cache_k.transpose(0, 1, 3, 4, 2)
    cv_t = cache_v.transpose(0, 1, 3, 4, 2)
    cst = state_conv.transpose(0, 2, 1, 3)
    gf = final_norm_g.reshape(1, D_MODEL)
    row = lambda v: v.reshape(1, -1)

    pc, ph, sc, sh = [], [], [], []
    k_all = v_all = k_last = v_last = None
    w_in_b = w_in.astype(BF16)
    wo_b = w_out.astype(BF16)
    wg_b = w_ffn_gate.astype(BF16)
    wu_b = w_ffn_up.astype(BF16)
    wd_b = w_ffn_down.astype(BF16)
    for l in range(depth):
        gates = _gate_weights(w_gate_a[l], w_gate_x[l])
        common = (conv_w[l], row(conv_b[l]), gates, row(b_gate_a[l]), row(b_gate_x[l]),
                  row(lru_lambda[l]))
        final = l == depth - 1

        qkv1, qkv4, qkv16, rnn, k_last, v_last, tail, h_last = _front_call(
            xp, row(norm1_g[l]), w_in_b, rope_p, *common, k_last, v_last, layer=l)
        att, k_all, v_all = _attn_call(qkv1, qkv4, qkv16, ck_t, cv_t, k_all, v_all, layer=l, nt=nt)
        xp = _ffn_call(xp, att, rnn, wo_b, row(norm2_g[l]), wg_b, wu_b, wd_b, gf,
                       layer=l, final_norm=final, tm=TM_FFN)
        pc.append(tail[:, SUBLANES - 1, :].reshape(1, CONV_WIDTH - 1, D_RNN))
        ph.append(h_last[0:1])

        q_s, k_s, v_s, rnn_s, nconv, hl = _sample_front_call(
            xs, row(norm1_g[l]), w_in_b, rope_s, *common, cst[l], state_h[l],
            layer=l, nb=nb, nt=nt)

        def split(v):
            return v.reshape(nt, nb, N_HEADS, HEAD_DIM)

        q_b = jnp.pad(split(q_s).transpose(1, 2, 0, 3), ((0, 0), (0, 0), (0, SUBLANES - nt), (0, 0)))

        def fresh(v):
            return jnp.pad(v.reshape(nt, nb, D_ATT).transpose(1, 0, 2),
                           ((0, 0), (0, SUBLANES - nt), (0, 0)))

        k_all, v_all, att_s = _sample_attn_call(
            q_b, ck_t, cv_t, fresh(k_s), fresh(v_s), k_all, v_all, layer=l, nt=nt)
        att_s = att_s[:, :, :nt].transpose(2, 0, 1, 3).reshape(nt * nb, D_ATT).astype(BF16)
        xs = _ffn_call(xs, att_s, rnn_s, wo_b, row(norm2_g[l]), wg_b, wu_b, wd_b, gf,
                       layer=l, final_norm=final, tm=nt * nb)
        sc.append(nconv.transpose(1, 0, 2))
        sh.append(hl)

    y_prompt = xp.reshape(1, s, D_MODEL)
    y_sample = xs.reshape(nt, nb, D_MODEL).transpose(1, 0, 2)
    sample_k = k_all.transpose(0, 1, 4, 2, 3)
    sample_v = v_all.transpose(0, 1, 4, 2, 3)
    window = (depth, 1, N_HEADS, HEAD_DIM, MAX_WINDOW)
    prompt_k = k_last.reshape(window).transpose(0, 1, 4, 2, 3)
    prompt_v = v_last.reshape(window).transpose(0, 1, 4, 2, 3)
    return (y_prompt, y_sample, prompt_k, prompt_v, jnp.stack(pc), jnp.stack(ph),
            sample_k, sample_v, jnp.stack(sc), jnp.stack(sh))
```

```python
import functools
import math

import numpy as np
import jax
import jax.numpy as jnp
from jax import lax
from jax.experimental import pallas as pl
from jax.experimental.pallas import tpu as pltpu

F32 = jnp.float32
BF16 = jnp.bfloat16

D_MODEL = 1024
HEAD_DIM = 64
D_ATT = 512
N_HEADS = 8
D_RNN = 512
N_RNN_BLOCKS = 8
CONV_WIDTH = 4
LRU_C = 8.0
DILATIONS = (1, 4, 16)
BAND = 128
MAX_WINDOW = 2048
ROPE_THETA = 500000.0
ROT_DIM = 16
D_FF = 2816
D_IN = 3 * D_ATT + 2 * D_RNN
RMS_EPS = 1e-6
PAST_LEN = 16384

LANES = 128
SUBLANES = 8
N_SLABS = D_ATT // LANES
FF_CHUNK = 256
NEG_BIG = -1e30
VMEM_LIMIT = 56 * 1024 * 1024

TM_FRONT = 512
TM_FFN = 256
ATT_BLOCK = MAX_WINDOW
ATT_STEPS = ATT_BLOCK // BAND
ATT_SUB = 2
CACHE_HEADS = 8
QKV_SLOT = (2, 0, 1)


def _dot(a, b):
    return jnp.dot(a, b, preferred_element_type=F32)


def _dot_nt(a, b):
    return lax.dot_general(a, b, (((1,), (1,)), ((), ())), preferred_element_type=F32)


def _rmsnorm(x, g):
    return x * lax.rsqrt(jnp.mean(x * x, axis=-1, keepdims=True) + RMS_EPS) * g


def _softplus(x):
    return jnp.maximum(x, 0.0) + jnp.log1p(jnp.exp(-jnp.abs(x)))


def _gelu_tanh(x):
    return 0.5 * x * (1.0 + jnp.tanh(math.sqrt(2.0 / math.pi) * (x + 0.044715 * (x * x * x))))


def _rope_slab(x, c, t):
    half = ROT_DIM // 2
    lane = lax.broadcasted_iota(jnp.int32, x.shape, 1)
    partner = jnp.where((lane & half) == 0, pltpu.roll(x, LANES - half, 1), pltpu.roll(x, half, 1))
    return x * c + partner * t


def _lru_coeffs(xc, wg_ref, bga, bgx, lam):
    xb = xc.astype(BF16)
    half = D_RNN // 2
    g0 = _dot(xb[:, :half], wg_ref[0])
    g1 = _dot(xb[:, half:], wg_ref[1])
    r = jax.nn.sigmoid(jnp.concatenate([g0[:, :half], g1[:, :half]], axis=1) + bga)
    ig = jax.nn.sigmoid(jnp.concatenate([g0[:, half:], g1[:, half:]], axis=1) + bgx)
    log_a = (-LRU_C) * r * _softplus(-lam)
    a = jnp.exp(log_a)
    th = jnp.tanh(log_a)
    b = jnp.sqrt(-2.0 * th / (1.0 - th)) * (ig * xc)
    return a, b


def _shift_rows(x, k, fill):
    row = lax.broadcasted_iota(jnp.int32, x.shape, 0)
    return jnp.where(row >= k, pltpu.roll(x, k, 0), fill)


def _front_kernel(x_ref, g1_ref, w_ref, cos_ref, sin_ref, cw_ref, cb_ref, wg_ref,
                  bga_ref, bgx_ref, lam_ref,
                  *refs, tm, aliased):
    if aliased:
        refs = refs[2:]
    qkv1_ref, qkv4_ref, qkv16_ref, rnn_ref, klast_ref, vlast_ref, tail_ref, hlast_ref = refs[:8]
    xs_s, hn_s, cls_s, nat_s, rn_s, carry_s, h_s = refs[8:]
    i = pl.program_id(0)
    grp = tm // 4
    n16 = tm // 16

    @pl.when(i == 0)
    def _():
        carry_s[...] = jnp.zeros_like(carry_s)
        h_s[...] = jnp.zeros_like(h_s)

    hn32 = _rmsnorm(x_ref[...], g1_ref[...])
    n_x = D_MODEL // LANES
    for c in range(n_x):
        xs_s[c] = hn32[:, c * LANES:(c + 1) * LANES]
    for j in range(4):
        for c in range(n_x):
            hn_s[j * grp:(j + 1) * grp, c * LANES:(c + 1) * LANES] = (
                xs_s[c, pl.ds(j, grp, stride=4), :].astype(BF16))

    xr = _dot(hn_s[...], w_ref[:, 3 * D_ATT:3 * D_ATT + D_RNN])
    yg = _dot(hn_s[...], w_ref[:, 3 * D_ATT + D_RNN:])
    slab = [xr[j * grp:(j + 1) * grp, :] for j in range(4)]
    back = [None] + [_shift_rows(slab[j], 1, carry_s[j - 1, SUBLANES - 1:SUBLANES, :])
                     for j in range(1, 4)]
    for j in range(1, 4):
        carry_s[j - 1] = slab[j][grp - SUBLANES:, :]
        tail_ref[j - 1] = slab[j][grp - SUBLANES:, :]
    w = [cw_ref[k:k + 1, :] for k in range(CONV_WIDTH)]
    taps = ((slab[0], back[3], back[2], back[1]),
            (slab[1], slab[0], back[3], back[2]),
            (slab[2], slab[1], slab[0], back[3]),
            (slab[3], slab[2], slab[1], slab[0]))
    xc = jnp.concatenate(
        [cb_ref[...] + t[0] * w[3] + t[1] * w[2] + t[2] * w[1] + t[3] * w[0] for t in taps], axis=0)
    a, b = _lru_coeffs(xc, wg_ref, bga_ref[...], bgx_ref[...], lam_ref[...])

    pp = [a[0:grp, :]]
    hh = [b[0:grp, :]]
    for j in range(1, 4):
        aj = a[j * grp:(j + 1) * grp, :]
        hh.append(aj * hh[j - 1] + b[j * grp:(j + 1) * grp, :])
        pp.append(aj * pp[j - 1])
    pc, hc = pp[3], hh[3]
    row = lax.broadcasted_iota(jnp.int32, (grp, D_RNN), 0)
    k = 1
    while k < grp:
        keep = row >= k
        hc = jnp.where(keep, hc + pc * pltpu.roll(hc, k, 0), hc)
        pc = jnp.where(keep, pc * pltpu.roll(pc, k, 0), pc)
        k *= 2
    h_in = h_s[...]
    ends = hc + pc * h_in
    h_prev = _shift_rows(ends, 1, h_in)
    h_s[...] = ends[grp - 1:grp, :]
    hlast_ref[...] = jnp.broadcast_to(ends[grp - 1:grp, :], (SUBLANES, D_RNN))
    for j in range(4):
        gated = (hh[j] + pp[j] * h_prev) * _gelu_tanh(yg[j * grp:(j + 1) * grp, :])
        for c in range(N_SLABS):
            rn_s[c, pl.ds(j, grp, stride=4), :] = gated[:, c * LANES:(c + 1) * LANES]
    for c in range(N_SLABS):
        rnn_ref[:, c * LANES:(c + 1) * LANES] = rn_s[c].astype(BF16)

    cos = cos_ref[...]
    sin = sin_ref[...]
    for part in range(3):
        z = _dot(hn_s[...], w_ref[:, part * D_ATT:(part + 1) * D_ATT])
        for c in range(N_SLABS):
            zc = z[:, c * LANES:(c + 1) * LANES]
            if part < 2:
                zc = _rope_slab(zc, cos, sin)
            if part == 0:
                zc = zc * (1.0 / math.sqrt(HEAD_DIM))
            n = QKV_SLOT[part] * N_SLABS + c
            cols = slice(n * LANES, (n + 1) * LANES)
            cls_s[n] = zc
            for j in range(4):
                rows = zc[j * grp:(j + 1) * grp, :]
                qkv4_ref[0, j, :, cols] = rows.astype(BF16)
                nat_s[n, pl.ds(j, grp, stride=4), :] = rows
            for j in range(4):
                for jj in range(4):
                    rows = cls_s[n, pl.ds(j * grp + jj, n16, stride=4), :]
                    qkv16_ref[0, j + 4 * jj, :, cols] = rows.astype(BF16)
            qkv1_ref[:, cols] = nat_s[n].astype(BF16)

    @pl.when(i >= pl.num_programs(0) - MAX_WINDOW // tm)
    def _():
        for c in range(N_SLABS):
            rows = slice(c * LANES, (c + 1) * LANES)
            klast_ref[rows, :] = nat_s[QKV_SLOT[1] * N_SLABS + c].T
            vlast_ref[rows, :] = nat_s[QKV_SLOT[2] * N_SLABS + c].T


def _class_major_positions(s, tm):
    r = np.arange(s)
    tile, within = r // tm, r % tm
    grp = tm // 4
    return (tile * tm + 4 * (within % grp) + within // grp).astype(np.int32)


def _front_call(x, g1, w_in, rope, cw, cb, wg, bga, bgx, lam, k_last, v_last, *, layer):
    s = x.shape[0]
    depth = w_in.shape[0]
    tm = TM_FRONT
    nt = s // tm
    n_last = MAX_WINDOW // tm
    row = lambda i: (i, 0)
    const2 = lambda i: (0, 0)
    const3 = lambda i: (0, 0, 0)
    last = lambda i: (layer, 0, jnp.maximum(i - (nt - n_last), 0))
    t4 = 4 * BAND // tm
    t16 = ATT_BLOCK // tm
    out_shape = (
        jax.ShapeDtypeStruct((s, 3 * D_ATT), BF16),
        jax.ShapeDtypeStruct((s // (4 * BAND), 4, BAND, 3 * D_ATT), BF16),
        jax.ShapeDtypeStruct((s // ATT_BLOCK, ATT_STEPS, BAND, 3 * D_ATT), BF16),
        jax.ShapeDtypeStruct((s, D_RNN), BF16),
        jax.ShapeDtypeStruct((depth, D_ATT, MAX_WINDOW), F32),
        jax.ShapeDtypeStruct((depth, D_ATT, MAX_WINDOW), F32),
        jax.ShapeDtypeStruct((CONV_WIDTH - 1, SUBLANES, D_RNN), F32),
        jax.ShapeDtypeStruct((SUBLANES, D_RNN), F32),
    )
    out_specs = (
        pl.BlockSpec((tm, 3 * D_ATT), row),
        pl.BlockSpec((1, 4, tm // 4, 3 * D_ATT), lambda i: (i // t4, 0, i % t4, 0)),
        pl.BlockSpec((1, ATT_STEPS, tm // ATT_STEPS, 3 * D_ATT), lambda i: (i // t16, 0, i % t16, 0)),
        pl.BlockSpec((tm, D_RNN), row),
        pl.BlockSpec((None, D_ATT, tm), last),
        pl.BlockSpec((None, D_ATT, tm), last),
        pl.BlockSpec((CONV_WIDTH - 1, SUBLANES, D_RNN), const3),
        pl.BlockSpec((SUBLANES, D_RNN), const2),
    )
    in_specs = [
        pl.BlockSpec((tm, D_MODEL), row),
        pl.BlockSpec((1, D_MODEL), const2),
        pl.BlockSpec((None, D_MODEL, D_IN), lambda i: (layer, 0, 0), pipeline_mode=pl.Buffered(1)),
        pl.BlockSpec((tm, LANES), lambda i: (i, 0)),
        pl.BlockSpec((tm, LANES), lambda i: (i, 1)),
        pl.BlockSpec((CONV_WIDTH, D_RNN), const2),
        pl.BlockSpec((1, D_RNN), const2),
        pl.BlockSpec((2, D_RNN // 2, D_RNN), const3),
        pl.BlockSpec((1, D_RNN), const2),
        pl.BlockSpec((1, D_RNN), const2),
        pl.BlockSpec((1, D_RNN), const2),
    ]
    scratch = [
        pltpu.VMEM((D_MODEL // LANES, tm, LANES), F32),
        pltpu.VMEM((tm, D_MODEL), BF16),
        pltpu.VMEM((3 * N_SLABS, tm, LANES), F32),
        pltpu.VMEM((3 * N_SLABS, tm, LANES), F32),
        pltpu.VMEM((N_SLABS, tm, LANES), F32),
        pltpu.VMEM((CONV_WIDTH - 1, SUBLANES, D_RNN), F32),
        pltpu.VMEM((1, D_RNN), F32),
    ]
    args = [x, g1, w_in, rope, rope, cw, cb, wg, bga, bgx, lam]
    aliases = {}
    aliased = k_last is not None
    if aliased:
        aliases = {len(args): 4, len(args) + 1: 5}
        in_specs += [pl.BlockSpec(memory_space=pl.ANY), pl.BlockSpec(memory_space=pl.ANY)]
        args += [k_last, v_last]
    return pl.pallas_call(
        functools.partial(_front_kernel, tm=tm, aliased=aliased),
        out_shape=out_shape,
        grid=(nt,),
        in_specs=in_specs,
        out_specs=out_specs,
        scratch_shapes=scratch,
        input_output_aliases=aliases,
        compiler_params=pltpu.CompilerParams(
            dimension_semantics=("arbitrary",), vmem_limit_bytes=VMEM_LIMIT),
        name="prompt_front",
    )(*args)


def _band_bias():
    a = np.arange(BAND)[:, None]
    c = np.arange(2 * BAND)[None, :]
    dist = BAND + a - c
    ok = (dist >= 0) & (dist <= BAND)
    with_prev = np.where(ok, 0.0, NEG_BIG)
    no_prev = np.where(ok & (c >= BAND), 0.0, NEG_BIG)
    return np.tile(np.stack([with_prev, no_prev]), (1, 2, 1)).astype(np.float32)


def _merge_rows(hp, rows, o_new, m_new, l_new, acc_s, m_s, l_s):
    m_old = m_s[hp, rows, :]
    m_tot = jnp.maximum(m_old, m_new)
    w_old = jnp.exp(m_old - m_tot)
    w_new = jnp.exp(m_new - m_tot)
    m_s[hp, rows, :] = m_tot
    l_s[hp, rows, :] = l_s[hp, rows, :] * w_old + l_new * w_new
    acc_s[hp, rows, :] = acc_s[hp, rows, :] * w_old + o_new * w_new


def _attn_kernel(bias_ref, *refs):
    pat_refs = [refs[3 * g:3 * g + 3] for g in range(3)]
    o_ref = refs[9]
    acc_s, m_s, l_s, tmp_s, nat_s = refs[10:15]
    blk = pl.program_id(0)
    step = pl.program_id(1)
    quarter = ATT_BLOCK // 4
    piece = BAND // 4

    @pl.when(step == 0)
    def _():
        acc_s[...] = jnp.zeros_like(acc_s)
        l_s[...] = jnp.zeros_like(l_s)
        m_s[...] = jnp.full(m_s.shape, NEG_BIG, F32)

    lane = lax.broadcasted_iota(jnp.int32, (BAND, LANES), 1)
    low = lane < HEAD_DIM

    for sub in range(ATT_SUB):
        s = step * ATT_SUB + sub
        first = (blk * ATT_STEPS + s == 0, blk * 4 + s // 4 == 0, blk == 0)
        base4 = (s % 4) * quarter
        rows = (None,
                pl.ds(pl.multiple_of(base4 + (s // 4) * BAND, BAND), BAND),
                pl.ds(base4 + s // 4, BAND, stride=4))
        for g in range(3):
            q_ref, prev_ref, cur_ref = pat_refs[g]
            bias = bias_ref[jnp.where(first[g], 1, 0)]
            for hp in range(N_SLABS):
                kcols = slice(hp * LANES, (hp + 1) * LANES)
                vcols = slice(D_ATT + hp * LANES, D_ATT + (hp + 1) * LANES)
                if g == 0:
                    here = slice(sub * BAND, (sub + 1) * BAND)
                    before = slice((sub - 1) * BAND, sub * BAND)
                    q2 = q_ref[here, kcols]
                    kp, vp = ((prev_ref[:, kcols], prev_ref[:, vcols]) if sub == 0 else
                              (cur_ref[before, kcols], cur_ref[before, vcols]))
                    kc, vc = cur_ref[here, kcols], cur_ref[here, vcols]
                else:
                    q2 = q_ref[sub, :, kcols]
                    kp, vp = prev_ref[sub, :, kcols], prev_ref[sub, :, vcols]
                    kc, vc = cur_ref[sub, :, kcols], cur_ref[sub, :, vcols]
                k2 = jnp.concatenate([kp, kc], axis=0)
                v2 = jnp.concatenate([vp, vc], axis=0)
                zero = jnp.zeros_like(q2)
                qq = jnp.concatenate([jnp.where(low, q2, zero), jnp.where(low, zero, q2)], axis=0)
                sc = _dot_nt(qq, k2) + bias
                mx = jnp.max(sc, axis=1, keepdims=True)
                p = jnp.exp(sc - mx)
                sm = jnp.sum(p, axis=1, keepdims=True)
                oo = _dot(p.astype(BF16), v2)
                o_new = jnp.where(low, oo[:BAND], oo[BAND:])
                m_new = jnp.where(low, mx[:BAND], mx[BAND:])
                l_new = jnp.where(low, sm[:BAND], sm[BAND:])
                if g == 0:
                    slot = 3 * (sub * N_SLABS + hp)
                    for n, val in enumerate((o_new, m_new, l_new)):
                        tmp_s[slot + n] = val
                    for c4 in range(4):
                        pick = pl.ds(c4, piece, stride=4)
                        dst = pl.ds(pl.multiple_of(c4 * quarter + s * piece, piece), piece)
                        _merge_rows(hp, dst, tmp_s[slot, pick, :], tmp_s[slot + 1, pick, :],
                                    tmp_s[slot + 2, pick, :], acc_s, m_s, l_s)
                else:
                    _merge_rows(hp, rows[g], o_new, m_new, l_new, acc_s, m_s, l_s)

    @pl.when(step == pl.num_programs(1) - 1)
    def _():
        for hp in range(N_SLABS):
            for c4 in range(4):
                part = slice(c4 * quarter, (c4 + 1) * quarter)
                nat_s[pl.ds(c4, quarter, stride=4), :] = acc_s[hp, part, :] / l_s[hp, part, :]
            o_ref[:, hp * LANES:(hp + 1) * LANES] = nat_s[...].astype(BF16)


def _attn_call(qkv1, qkv4, qkv16):
    s = qkv1.shape[0]
    nb = s // ATT_BLOCK
    steps = ATT_STEPS // ATT_SUB
    per4 = 4 // ATT_SUB
    wide = 3 * D_ATT
    q1 = qkv1.reshape(s // BAND, BAND, wide)
    q1_sub = qkv1.reshape(s // (ATT_SUB * BAND), ATT_SUB * BAND, wide)

    def width(is_q):
        return D_ATT if is_q else 2 * D_ATT

    def col(is_q):
        return QKV_SLOT[0] if is_q else 0

    def spec1(is_q, prev):
        if prev:
            return pl.BlockSpec(
                (None, BAND, width(is_q)),
                lambda b, t: (jnp.maximum((b * steps + t) * ATT_SUB - 1, 0), 0, col(is_q)))
        return pl.BlockSpec((None, ATT_SUB * BAND, width(is_q)),
                            lambda b, t: (b * steps + t, 0, col(is_q)))

    def spec4(is_q, prev):
        def im(b, t):
            j = b * 4 + t // per4
            return (jnp.maximum(j - 1, 0) if prev else j, t % per4, 0, col(is_q))
        return pl.BlockSpec((None, ATT_SUB, BAND, width(is_q)), im)

    def spec16(is_q, prev):
        def im(b, t):
            return (jnp.maximum(b - 1, 0) if prev else b, t, 0, col(is_q))
        return pl.BlockSpec((None, ATT_SUB, BAND, width(is_q)), im)

    in_specs = [pl.BlockSpec((2, 2 * BAND, 2 * BAND), lambda b, t: (0, 0, 0))]
    args = [jnp.asarray(_band_bias())]
    in_specs += [spec1(True, False), spec1(False, True), spec1(False, False)]
    args += [q1_sub, q1, q1_sub]
    for arr, mk in ((qkv4, spec4), (qkv16, spec16)):
        in_specs += [mk(True, False), mk(False, True), mk(False, False)]
        args += [arr] * 3
    state = pltpu.VMEM((N_SLABS, ATT_BLOCK, LANES), F32)
    return pl.pallas_call(
        _attn_kernel,
        out_shape=jax.ShapeDtypeStruct((s, D_ATT), BF16),
        grid=(nb, steps),
        in_specs=in_specs,
        out_specs=pl.BlockSpec((ATT_BLOCK, D_ATT), lambda b, t: (b, 0)),
        scratch_shapes=[state, state, state,
                        pltpu.VMEM((3 * ATT_SUB * N_SLABS, BAND, LANES), F32),
                        pltpu.VMEM((ATT_BLOCK, LANES), F32)],
        compiler_params=pltpu.CompilerParams(
            dimension_semantics=("arbitrary", "arbitrary"), vmem_limit_bytes=VMEM_LIMIT),
        name="prompt_attention",
    )(*args)


def _ffn_kernel(x_ref, att_ref, rnn_ref, wo_ref, g2_ref, wg_ref, wu_ref, wd_ref, gf_ref,
                *refs, final_norm, slide_nt, aliased):
    if slide_nt:
        kc_ref, vc_ref = refs[:2]
        o_ref, ko_ref, vo_ref, act_s = refs[(4 if aliased else 2):]
        w_buf = kc_ref.shape[-1]
        for h in range(kc_ref.shape[0]):
            ko_ref[h] = pltpu.roll(kc_ref[h], w_buf - slide_nt, 1)
            vo_ref[h] = pltpu.roll(vc_ref[h], w_buf - slide_nt, 1)
    else:
        o_ref, act_s = refs
    y = (x_ref[...] + _dot(att_ref[...], wo_ref[0:D_ATT, :])
         + _dot(rnn_ref[...], wo_ref[D_ATT:D_MODEL, :]))
    hf = _rmsnorm(y, g2_ref[...]).astype(BF16)
    for c in range(D_FF // FF_CHUNK):
        cols = slice(c * FF_CHUNK, (c + 1) * FF_CHUNK)
        gate = _dot(hf, wg_ref[:, cols])
        up = _dot(hf, wu_ref[:, cols])
        act_s[:, cols] = (gate * jax.nn.sigmoid(gate) * up).astype(BF16)
    out = y + _dot(act_s[...], wd_ref[...])
    if final_norm:
        out = _rmsnorm(out, gf_ref[...])
    o_ref[...] = out


def _ffn_call(x, att, rnn, wo, g2, wg, wu, wd, gf, *, layer, final_norm, tm, slide=None):
    s = x.shape[0]
    steps = s // tm
    row = lambda i: (i, 0)
    const = lambda i: (0, 0)
    pick = lambda i: (layer, 0, 0)
    once = pl.Buffered(1)
    in_specs = [
        pl.BlockSpec((tm, D_MODEL), row),
        pl.BlockSpec((tm, D_ATT), row),
        pl.BlockSpec((tm, D_RNN), row),
        pl.BlockSpec((None, D_MODEL, D_MODEL), pick, pipeline_mode=once),
        pl.BlockSpec((1, D_MODEL), const),
        pl.BlockSpec((None, D_MODEL, D_FF), pick, pipeline_mode=once),
        pl.BlockSpec((None, D_MODEL, D_FF), pick, pipeline_mode=once),
        pl.BlockSpec((None, D_FF, D_MODEL), pick, pipeline_mode=once),
        pl.BlockSpec((1, D_MODEL), const),
    ]
    args = [x, att, rnn, wo, g2, wg, wu, wd, gf]
    out_shape = jax.ShapeDtypeStruct((s, D_MODEL), F32)
    out_specs = pl.BlockSpec((tm, D_MODEL), row)
    aliases = {}
    aliased = False
    if slide is not None:
        kc, vc, k_all, v_all, nt = slide
        shape5 = kc.shape
        depth, n_batch, n_heads, _, w_buf = shape5
        units = n_batch * n_heads
        hb = units // steps
        assert hb * steps == units
        flat = (depth, units, HEAD_DIM, w_buf)
        blk_c = pl.BlockSpec((None, hb, HEAD_DIM, w_buf), lambda i: (layer, i, 0, 0))
        in_specs += [blk_c, blk_c]
        args += [kc.reshape(flat), vc.reshape(flat)]
        aliased = k_all is not None
        if aliased:
            aliases = {len(args): 1, len(args) + 1: 2}
            in_specs += [pl.BlockSpec(memory_space=pl.ANY), pl.BlockSpec(memory_space=pl.ANY)]
            args += [k_all.reshape(flat), v_all.reshape(flat)]
        full = jax.ShapeDtypeStruct(flat, F32)
        out_shape = (out_shape, full, full)
        out_specs = (out_specs, blk_c, blk_c)
    res = pl.pallas_call(
        functools.partial(_ffn_kernel, final_norm=final_norm,
                          slide_nt=slide[4] if slide is not None else 0, aliased=aliased),
        out_shape=out_shape,
        grid=(steps,),
        in_specs=in_specs,
        out_specs=out_specs,
        input_output_aliases=aliases,
        scratch_shapes=[pltpu.VMEM((tm, D_FF), BF16)],
        compiler_params=pltpu.CompilerParams(
            dimension_semantics=("arbitrary",), vmem_limit_bytes=VMEM_LIMIT),
        name="out_proj_ffn",
    )(*args)
    if slide is None:
        return res
    out, k_all, v_all = res
    return out, k_all.reshape(shape5), v_all.reshape(shape5)


def _sample_front_kernel(x_ref, g1_ref, w_ref, cos_ref, sin_ref, cw_ref, cb_ref, wg_ref,
                         bga_ref, bgx_ref, lam_ref, cst_ref, h0_ref,
                         q_ref, k_ref, v_ref, rnn_ref, nconv_ref, hl_ref, *, nb, nt):
    hn = _rmsnorm(x_ref[...], g1_ref[...]).astype(BF16)
    cos = cos_ref[...]
    sin = sin_ref[...]
    for part, out in enumerate((q_ref, k_ref, v_ref)):
        z = _dot(hn, w_ref[:, part * D_ATT:(part + 1) * D_ATT])
        for c in range(N_SLABS):
            zc = z[:, c * LANES:(c + 1) * LANES]
            if part < 2:
                zc = _rope_slab(zc, cos, sin)
            if part == 0:
                zc = zc * (1.0 / math.sqrt(HEAD_DIM))
            out[:, c * LANES:(c + 1) * LANES] = zc
    xr = _dot(hn, w_ref[:, 3 * D_ATT:3 * D_ATT + D_RNN])
    yg = _dot(hn, w_ref[:, 3 * D_ATT + D_RNN:])
    xpad = [cst_ref[j] for j in range(CONV_WIDTH - 1)]
    xpad += [xr[t * nb:(t + 1) * nb, :] for t in range(nt)]
    xc = []
    for t in range(nt):
        acc = cb_ref[...] + xpad[t] * cw_ref[0:1, :]
        for j in range(1, CONV_WIDTH):
            acc = acc + xpad[t + j] * cw_ref[j:j + 1, :]
        xc.append(acc)
    for j in range(CONV_WIDTH - 1):
        nconv_ref[j] = xpad[nt + j]
    a, b = _lru_coeffs(jnp.concatenate(xc, axis=0), wg_ref, bga_ref[...], bgx_ref[...],
                       lam_ref[...])
    h = h0_ref[...]
    hs = []
    for t in range(nt):
        h = a[t * nb:(t + 1) * nb, :] * h + b[t * nb:(t + 1) * nb, :]
        hs.append(h)
    hl_ref[...] = h
    rnn_ref[...] = (jnp.concatenate(hs, axis=0) * _gelu_tanh(yg)).astype(BF16)


def _sample_front_call(x, g1, w_in, rope, cw, cb, wg, bga, bgx, lam, cst, h0, *, layer, nb, nt):
    m = x.shape[0]

    def whole(a):
        return pl.BlockSpec(a.shape, lambda i, nd=a.ndim: (0,) * nd)

    in_specs = [whole(x), whole(g1),
                pl.BlockSpec((None, D_MODEL, D_IN), lambda i: (layer, 0, 0)),
                pl.BlockSpec((m, LANES), lambda i: (0, 0)), pl.BlockSpec((m, LANES), lambda i: (0, 1))]
    in_specs += [whole(a) for a in (cw, cb, wg, bga, bgx, lam, cst, h0)]
    out_shape = (
        jax.ShapeDtypeStruct((m, D_ATT), F32),
        jax.ShapeDtypeStruct((m, D_ATT), F32),
        jax.ShapeDtypeStruct((m, D_ATT), F32),
        jax.ShapeDtypeStruct((m, D_RNN), BF16),
        jax.ShapeDtypeStruct((CONV_WIDTH - 1, nb, D_RNN), F32),
        jax.ShapeDtypeStruct((nb, D_RNN), F32),
    )
    out_specs = tuple(pl.BlockSpec(o.shape, lambda i, nd=len(o.shape): (0,) * nd) for o in out_shape)
    return pl.pallas_call(
        functools.partial(_sample_front_kernel, nb=nb, nt=nt),
        out_shape=out_shape,
        grid=(1,),
        in_specs=in_specs,
        out_specs=out_specs,
        compiler_params=pltpu.CompilerParams(
            dimension_semantics=("arbitrary",), vmem_limit_bytes=VMEM_LIMIT),
        name="sample_front",
    )(x, g1, w_in, rope, rope, cw, cb, wg, bga, bgx, lam, cst, h0)


def _sample_mult(nt, w_buf):
    cache = np.zeros((SUBLANES, w_buf), np.float32)
    new = np.zeros((SUBLANES, LANES), np.float32)
    for dil in DILATIONS:
        for m in range(BAND + 1):
            idx = w_buf + np.arange(nt) - dil * m
            for t in range(nt):
                if idx[t] < 0:
                    continue
                if idx[t] < w_buf:
                    cache[t, idx[t]] += 1.0
                else:
                    new[t, LANES - nt + (idx[t] - w_buf)] += 1.0
    cache[nt:] = 1.0
    return cache, new


def _sample_attn_kernel(q_ref, kc_ref, vc_ref, kn_ref, vn_ref, mc_ref, mn_ref, k_any, v_any,
                        ko_ref, vo_ref, att_ref, *, nt):
    del k_any, v_any
    w_buf = kc_ref.shape[-1]
    mc = mc_ref[...]
    mn = mn_ref[...]
    lane = lax.broadcasted_iota(jnp.int32, (HEAD_DIM, LANES), 1)
    fresh = lane >= LANES - nt

    def tail_tiles(new_ref, pair):
        rows = jnp.concatenate([new_ref[0, :, pair * LANES:(pair + 1) * LANES],
                                jnp.zeros((LANES - SUBLANES, LANES), F32)], axis=0)
        placed = pltpu.roll(rows.T, LANES - nt, 1)
        return placed[:HEAD_DIM], placed[HEAD_DIM:]

    kn_tiles, vn_tiles = [], []
    for pair in range(CACHE_HEADS // 2):
        kn_tiles += tail_tiles(kn_ref, pair)
        vn_tiles += tail_tiles(vn_ref, pair)
    for h in range(CACHE_HEADS):
        q = q_ref[0, h].astype(BF16)
        kc = kc_ref[0, 0, h]
        vc = vc_ref[0, 0, h]
        kn = kn_tiles[h]
        vn = vn_tiles[h]
        sc = jnp.where(mc > 0, _dot(q, kc.astype(BF16)), NEG_BIG)
        sn = jnp.where(mn > 0, _dot(q, kn.astype(BF16)), NEG_BIG)
        mx = jnp.maximum(jnp.max(sc, axis=1, keepdims=True), jnp.max(sn, axis=1, keepdims=True))
        pc = mc * jnp.exp(sc - mx)
        pn = mn * jnp.exp(sn - mx)
        den = jnp.sum(pc, axis=1, keepdims=True) + jnp.sum(pn, axis=1, keepdims=True)
        o = _dot_nt(pc.astype(BF16), vc.astype(BF16)) + _dot_nt(pn.astype(BF16), vn.astype(BF16))
        att_ref[0, h] = o / den
        for src, new, dst in ((kc, kn, ko_ref), (vc, vn, vo_ref)):
            moved = pltpu.roll(src[:, w_buf - 2 * LANES:], 2 * LANES - nt, 1)[:, LANES:]
            dst[0, 0, h] = jnp.where(fresh, new, moved)


def _sample_attn_call(q, kc, vc, kn, vn, k_all, v_all, *, layer, nt):
    _, nb, nh, _, w_buf = kc.shape
    hb = CACHE_HEADS
    mc, mn = _sample_mult(nt, w_buf)
    blk_c = pl.BlockSpec((1, 1, hb, HEAD_DIM, w_buf), lambda b, h: (layer, b, h, 0, 0))
    blk_n = pl.BlockSpec((1, SUBLANES, hb * HEAD_DIM), lambda b, h: (b, 0, h))
    blk_q = pl.BlockSpec((1, hb, SUBLANES, HEAD_DIM), lambda b, h: (b, h, 0, 0))
    blk_o = pl.BlockSpec((1, 1, hb, HEAD_DIM, LANES), lambda b, h: (layer, b, h, 0, w_buf // LANES - 1))
    const = lambda b, h: (0, 0)
    in_specs = [blk_q, blk_c, blk_c, blk_n, blk_n,
                pl.BlockSpec((SUBLANES, w_buf), const), pl.BlockSpec((SUBLANES, LANES), const),
                pl.BlockSpec(memory_space=pl.ANY), pl.BlockSpec(memory_space=pl.ANY)]
    args = [q, kc, vc, kn, vn, jnp.asarray(mc), jnp.asarray(mn), k_all, v_all]
    aliases = {7: 0, 8: 1}
    full = jax.ShapeDtypeStruct(kc.shape, F32)
    return pl.pallas_call(
        functools.partial(_sample_attn_kernel, nt=nt),
        out_shape=(full, full, jax.ShapeDtypeStruct((nb, nh, SUBLANES, HEAD_DIM), F32)),
        grid=(nb, nh // hb),
        in_specs=in_specs,
        out_specs=(blk_o, blk_o, blk_q),
        input_output_aliases=aliases,
        compiler_params=pltpu.CompilerParams(
            dimension_semantics=("arbitrary", "arbitrary"), vmem_limit_bytes=VMEM_LIMIT),
        name="sample_attention",
    )(*args)


def _rope_tables(pos):
    half = ROT_DIM // 2
    inv = ROPE_THETA ** (-np.arange(half, dtype=np.float64) * 2.0 / ROT_DIM)
    ang = np.asarray(pos, np.float64)[:, None] * inv[None, :]
    cos, sin = np.cos(ang), np.sin(ang)
    n = ang.shape[0]
    rest = HEAD_DIM - ROT_DIM
    c = np.concatenate([cos, cos, np.ones((n, rest))], axis=1)
    t = np.concatenate([-sin, sin, np.zeros((n, rest))], axis=1)
    rep = LANES // HEAD_DIM
    return jnp.asarray(np.concatenate([np.tile(c, (1, rep)), np.tile(t, (1, rep))], axis=1), F32)


def _gate_weights(w_a, w_x):
    def dense(w):
        return jax.scipy.linalg.block_diag(*[w[n] for n in range(N_RNN_BLOCKS)])
    da, dx = dense(w_a), dense(w_x)
    half = D_RNN // 2
    chunks = [jnp.concatenate([da[c * half:(c + 1) * half, c * half:(c + 1) * half],
                               dx[c * half:(c + 1) * half, c * half:(c + 1) * half]], axis=1)
              for c in range(2)]
    return jnp.stack(chunks).astype(BF16)


def kernel(x_prompt, x_sample, cache_k, cache_v, state_conv, state_h, norm1_g, w_in, conv_w, conv_b,
           w_gate_a, b_gate_a, w_gate_x, b_gate_x, lru_lambda, w_out, norm2_g, w_ffn_gate, w_ffn_up,
           w_ffn_down, final_norm_g):
    bp, s, _ = x_prompt.shape
    nb, nt, _ = x_sample.shape
    depth = norm1_g.shape[0]
    w_buf = cache_k.shape[2]
    assert bp == 1 and s % ATT_BLOCK == 0 and w_buf == MAX_WINDOW and nt * nb == BAND

    xp = x_prompt.reshape(s, D_MODEL)
    xs = x_sample.transpose(1, 0, 2).reshape(nt * nb, D_MODEL)
    rope_p = _rope_tables(_class_major_positions(s, TM_FRONT))
    rope_s = _rope_tables(PAST_LEN + np.repeat(np.arange(nt), nb))
    ck_t = cache_k.transpose(0, 1, 3, 4, 2)
    cv_t = cache_v.transpose(0, 1, 3, 4, 2)
    cst = state_conv.transpose(0, 2, 1, 3)
    gf = final_norm_g.reshape(1, D_MODEL)
    row = lambda v: v.reshape(1, -1)

    pc, ph, sc, sh = [], [], [], []
    k_all = v_all = k_last = v_last = None
    w_in_b = w_in.astype(BF16)
    wo_b = w_out.astype(BF16)
    wg_b = w_ffn_gate.astype(BF16)
    wu_b = w_ffn_up.astype(BF16)
    wd_b = w_ffn_down.astype(BF16)
    for l in range(depth):
        gates = _gate_weights(w_gate_a[l], w_gate_x[l])
        common = (conv_w[l], row(conv_b[l]), gates, row(b_gate_a[l]), row(b_gate_x[l]),
                  row(lru_lambda[l]))
        final = l == depth - 1

        qkv1, qkv4, qkv16, rnn, k_last, v_last, tail, h_last = _front_call(
            xp, row(norm1_g[l]), w_in_b, rope_p, *common, k_last, v_last, layer=l)
        att = _attn_call(qkv1, qkv4, qkv16)
        xp, k_all, v_all = _ffn_call(xp, att, rnn, wo_b, row(norm2_g[l]), wg_b, wu_b, wd_b, gf,
                                     layer=l, final_norm=final, tm=TM_FFN,
                                     slide=(ck_t, cv_t, k_all, v_all, nt))
        pc.append(tail[:, SUBLANES - 1, :].reshape(1, CONV_WIDTH - 1, D_RNN))
        ph.append(h_last[0:1])

        q_s, k_s, v_s, rnn_s, nconv, hl = _sample_front_call(
            xs, row(norm1_g[l]), w_in_b, rope_s, *common, cst[l], state_h[l],
            layer=l, nb=nb, nt=nt)

        def split(v):
            return v.reshape(nt, nb, N_HEADS, HEAD_DIM)

        q_b = jnp.pad(split(q_s).transpose(1, 2, 0, 3), ((0, 0), (0, 0), (0, SUBLANES - nt), (0, 0)))

        def fresh(v):
            return jnp.pad(v.reshape(nt, nb, D_ATT).transpose(1, 0, 2),
                           ((0, 0), (0, SUBLANES - nt), (0, 0)))

        k_all, v_all, att_s = _sample_attn_call(
            q_b, ck_t, cv_t, fresh(k_s), fresh(v_s), k_all, v_all, layer=l, nt=nt)
        att_s = att_s[:, :, :nt].transpose(2, 0, 1, 3).reshape(nt * nb, D_ATT).astype(BF16)
        xs = _ffn_call(xs, att_s, rnn_s, wo_b, row(norm2_g[l]), wg_b, wu_b, wd_b, gf,
                       layer=l, final_norm=final, tm=nt * nb)
        sc.append(nconv.transpose(1, 0, 2))
        sh.append(hl)

    y_prompt = xp.reshape(1, s, D_MODEL)
    y_sample = xs.reshape(nt, nb, D_MODEL).transpose(1, 0, 2)
    sample_k = k_all.transpose(0, 1, 4, 2, 3)
    sample_v = v_all.transpose(0, 1, 4, 2, 3)
    window = (depth, 1, N_HEADS, HEAD_DIM, MAX_WINDOW)
    prompt_k = k_last.reshape(window).transpose(0, 1, 4, 2, 3)
    prompt_v = v_last.reshape(window).transpose(0, 1, 4, 2, 3)
    return (y_prompt, y_sample, prompt_k, prompt_v, jnp.stack(pc), jnp.stack(ph),
            sample_k, sample_v, jnp.stack(sc), jnp.stack(sh))
```

```python
import functools
import math

import numpy as np
import jax
import jax.numpy as jnp
from jax import lax
from jax.experimental import pallas as pl
from jax.experimental.pallas import tpu as pltpu

F32 = jnp.float32
BF16 = jnp.bfloat16

D_MODEL = 1024
HEAD_DIM = 64
D_ATT = 512
N_HEADS = 8
D_RNN = 512
N_RNN_BLOCKS = 8
CONV_WIDTH = 4
LRU_C = 8.0
DILATIONS = (1, 4, 16)
BAND = 128
MAX_WINDOW = 2048
ROPE_THETA = 500000.0
ROT_DIM = 16
D_FF = 2816
D_IN = 3 * D_ATT + 2 * D_RNN
RMS_EPS = 1e-6
PAST_LEN = 16384

LANES = 128
SUBLANES = 8
N_SLABS = D_ATT // LANES
FF_CHUNK = 256
NEG_BIG = -1e30
VMEM_LIMIT = 56 * 1024 * 1024

TM_FRONT = 512
TM_FFN = 256
ATT_BLOCK = MAX_WINDOW
ATT_STEPS = ATT_BLOCK // BAND
ATT_SUB = 4
CACHE_HEADS = 8
QKV_SLOT = (2, 0, 1)


def _dot(a, b):
    return jnp.dot(a, b, preferred_element_type=F32)


def _dot_nt(a, b):
    return lax.dot_general(a, b, (((1,), (1,)), ((), ())), preferred_element_type=F32)


def _rmsnorm(x, g):
    return x * lax.rsqrt(jnp.mean(x * x, axis=-1, keepdims=True) + RMS_EPS) * g


def _softplus(x):
    return jnp.maximum(x, 0.0) + jnp.log1p(jnp.exp(-jnp.abs(x)))


def _gelu_tanh(x):
    return 0.5 * x * (1.0 + jnp.tanh(math.sqrt(2.0 / math.pi) * (x + 0.044715 * (x * x * x))))


def _rope_slab(x, c, t):
    half = ROT_DIM // 2
    lane = lax.broadcasted_iota(jnp.int32, x.shape, 1)
    partner = jnp.where((lane & half) == 0, pltpu.roll(x, LANES - half, 1), pltpu.roll(x, half, 1))
    return x * c + partner * t


def _lru_coeffs(xc, wg_ref, bga, bgx, lam):
    xb = xc.astype(BF16)
    half = D_RNN // 2
    g0 = _dot(xb[:, :half], wg_ref[0])
    g1 = _dot(xb[:, half:], wg_ref[1])
    r = jax.nn.sigmoid(jnp.concatenate([g0[:, :half], g1[:, :half]], axis=1) + bga)
    ig = jax.nn.sigmoid(jnp.concatenate([g0[:, half:], g1[:, half:]], axis=1) + bgx)
    log_a = (-LRU_C) * r * _softplus(-lam)
    a = jnp.exp(log_a)
    th = jnp.tanh(log_a)
    b = jnp.sqrt(-2.0 * th / (1.0 - th)) * (ig * xc)
    return a, b


def _shift_rows(x, k, fill):
    row = lax.broadcasted_iota(jnp.int32, x.shape, 0)
    return jnp.where(row >= k, pltpu.roll(x, k, 0), fill)


def _front_kernel(x_ref, g1_ref, w_ref, cos_ref, sin_ref, cw_ref, cb_ref, wg_ref,
                  bga_ref, bgx_ref, lam_ref,
                  *refs, tm, aliased):
    if aliased:
        refs = refs[2:]
    qkv1_ref, qkv4_ref, qkv16_ref, rnn_ref, klast_ref, vlast_ref, tail_ref, hlast_ref = refs[:8]
    xs_s, hn_s, cls_s, nat_s, rn_s, carry_s, h_s = refs[8:]
    i = pl.program_id(0)
    grp = tm // 4
    n16 = tm // 16

    @pl.when(i == 0)
    def _():
        carry_s[...] = jnp.zeros_like(carry_s)
        h_s[...] = jnp.zeros_like(h_s)

    hn32 = _rmsnorm(x_ref[...], g1_ref[...])
    n_x = D_MODEL // LANES
    for c in range(n_x):
        xs_s[c] = hn32[:, c * LANES:(c + 1) * LANES]
    for j in range(4):
        for c in range(n_x):
            hn_s[j * grp:(j + 1) * grp, c * LANES:(c + 1) * LANES] = (
                xs_s[c, pl.ds(j, grp, stride=4), :].astype(BF16))

    xr = _dot(hn_s[...], w_ref[:, 3 * D_ATT:3 * D_ATT + D_RNN])
    yg = _dot(hn_s[...], w_ref[:, 3 * D_ATT + D_RNN:])
    slab = [xr[j * grp:(j + 1) * grp, :] for j in range(4)]
    back = [None] + [_shift_rows(slab[j], 1, carry_s[j - 1, SUBLANES - 1:SUBLANES, :])
                     for j in range(1, 4)]
    for j in range(1, 4):
        carry_s[j - 1] = slab[j][grp - SUBLANES:, :]
        tail_ref[j - 1] = slab[j][grp - SUBLANES:, :]
    w = [cw_ref[k:k + 1, :] for k in range(CONV_WIDTH)]
    taps = ((slab[0], back[3], back[2], back[1]),
            (slab[1], slab[0], back[3], back[2]),
            (slab[2], slab[1], slab[0], back[3]),
            (slab[3], slab[2], slab[1], slab[0]))
    xc = jnp.concatenate(
        [cb_ref[...] + t[0] * w[3] + t[1] * w[2] + t[2] * w[1] + t[3] * w[0] for t in taps], axis=0)
    a, b = _lru_coeffs(xc, wg_ref, bga_ref[...], bgx_ref[...], lam_ref[...])

    pp = [a[0:grp, :]]
    hh = [b[0:grp, :]]
    for j in range(1, 4):
        aj = a[j * grp:(j + 1) * grp, :]
        hh.append(aj * hh[j - 1] + b[j * grp:(j + 1) * grp, :])
        pp.append(aj * pp[j - 1])
    pc, hc = pp[3], hh[3]
    row = lax.broadcasted_iota(jnp.int32, (grp, D_RNN), 0)
    k = 1
    while k < grp:
        keep = row >= k
        hc = jnp.where(keep, hc + pc * pltpu.roll(hc, k, 0), hc)
        pc = jnp.where(keep, pc * pltpu.roll(pc, k, 0), pc)
        k *= 2
    h_in = h_s[...]
    ends = hc + pc * h_in
    h_prev = _shift_rows(ends, 1, h_in)
    h_s[...] = ends[grp - 1:grp, :]
    hlast_ref[...] = jnp.broadcast_to(ends[grp - 1:grp, :], (SUBLANES, D_RNN))
    for j in range(4):
        gated = (hh[j] + pp[j] * h_prev) * _gelu_tanh(yg[j * grp:(j + 1) * grp, :])
        for c in range(N_SLABS):
            rn_s[c, pl.ds(j, grp, stride=4), :] = gated[:, c * LANES:(c + 1) * LANES]
    for c in range(N_SLABS):
        rnn_ref[:, c * LANES:(c + 1) * LANES] = rn_s[c].astype(BF16)

    cos = cos_ref[...]
    sin = sin_ref[...]
    for part in range(3):
        z = _dot(hn_s[...], w_ref[:, part * D_ATT:(part + 1) * D_ATT])
        for c in range(N_SLABS):
            zc = z[:, c * LANES:(c + 1) * LANES]
            if part < 2:
                zc = _rope_slab(zc, cos, sin)
            if part == 0:
                zc = zc * (1.0 / math.sqrt(HEAD_DIM))
            n = QKV_SLOT[part] * N_SLABS + c
            cols = slice(n * LANES, (n + 1) * LANES)
            cls_s[n] = zc
            for j in range(4):
                rows = zc[j * grp:(j + 1) * grp, :]
                qkv4_ref[0, j, :, cols] = rows.astype(BF16)
                nat_s[n, pl.ds(j, grp, stride=4), :] = rows
            for j in range(4):
                for jj in range(4):
                    rows = cls_s[n, pl.ds(j * grp + jj, n16, stride=4), :]
                    qkv16_ref[0, j + 4 * jj, :, cols] = rows.astype(BF16)
            qkv1_ref[:, cols] = nat_s[n].astype(BF16)

    @pl.when(i >= pl.num_programs(0) - MAX_WINDOW // tm)
    def _():
        for c in range(N_SLABS):
            rows = slice(c * LANES, (c + 1) * LANES)
            klast_ref[rows, :] = nat_s[QKV_SLOT[1] * N_SLABS + c].T
            vlast_ref[rows, :] = nat_s[QKV_SLOT[2] * N_SLABS + c].T


def _class_major_positions(s, tm):
    r = np.arange(s)
    tile, within = r // tm, r % tm
    grp = tm // 4
    return (tile * tm + 4 * (within % grp) + within // grp).astype(np.int32)


def _front_call(x, g1, w_in, rope, cw, cb, wg, bga, bgx, lam, k_last, v_last, *, layer):
    s = x.shape[0]
    depth = w_in.shape[0]
    tm = TM_FRONT
    nt = s // tm
    n_last = MAX_WINDOW // tm
    row = lambda i: (i, 0)
    const2 = lambda i: (0, 0)
    const3 = lambda i: (0, 0, 0)
    last = lambda i: (layer, 0, jnp.maximum(i - (nt - n_last), 0))
    t4 = 4 * BAND // tm
    t16 = ATT_BLOCK // tm
    out_shape = (
        jax.ShapeDtypeStruct((s, 3 * D_ATT), BF16),
        jax.ShapeDtypeStruct((s // (4 * BAND), 4, BAND, 3 * D_ATT), BF16),
        jax.ShapeDtypeStruct((s // ATT_BLOCK, ATT_STEPS, BAND, 3 * D_ATT), BF16),
        jax.ShapeDtypeStruct((s, D_RNN), BF16),
        jax.ShapeDtypeStruct((depth, D_ATT, MAX_WINDOW), F32),
        jax.ShapeDtypeStruct((depth, D_ATT, MAX_WINDOW), F32),
        jax.ShapeDtypeStruct((CONV_WIDTH - 1, SUBLANES, D_RNN), F32),
        jax.ShapeDtypeStruct((SUBLANES, D_RNN), F32),
    )
    out_specs = (
        pl.BlockSpec((tm, 3 * D_ATT), row),
        pl.BlockSpec((1, 4, tm // 4, 3 * D_ATT), lambda i: (i // t4, 0, i % t4, 0)),
        pl.BlockSpec((1, ATT_STEPS, tm // ATT_STEPS, 3 * D_ATT), lambda i: (i // t16, 0, i % t16, 0)),
        pl.BlockSpec((tm, D_RNN), row),
        pl.BlockSpec((None, D_ATT, tm), last),
        pl.BlockSpec((None, D_ATT, tm), last),
        pl.BlockSpec((CONV_WIDTH - 1, SUBLANES, D_RNN), const3),
        pl.BlockSpec((SUBLANES, D_RNN), const2),
    )
    in_specs = [
        pl.BlockSpec((tm, D_MODEL), row),
        pl.BlockSpec((1, D_MODEL), const2),
        pl.BlockSpec((None, D_MODEL, D_IN), lambda i: (layer, 0, 0), pipeline_mode=pl.Buffered(1)),
        pl.BlockSpec((tm, LANES), lambda i: (i, 0)),
        pl.BlockSpec((tm, LANES), lambda i: (i, 1)),
        pl.BlockSpec((CONV_WIDTH, D_RNN), const2),
        pl.BlockSpec((1, D_RNN), const2),
        pl.BlockSpec((2, D_RNN // 2, D_RNN), const3),
        pl.BlockSpec((1, D_RNN), const2),
        pl.BlockSpec((1, D_RNN), const2),
        pl.BlockSpec((1, D_RNN), const2),
    ]
    scratch = [
        pltpu.VMEM((D_MODEL // LANES, tm, LANES), F32),
        pltpu.VMEM((tm, D_MODEL), BF16),
        pltpu.VMEM((3 * N_SLABS, tm, LANES), F32),
        pltpu.VMEM((3 * N_SLABS, tm, LANES), F32),
        pltpu.VMEM((N_SLABS, tm, LANES), F32),
        pltpu.VMEM((CONV_WIDTH - 1, SUBLANES, D_RNN), F32),
        pltpu.VMEM((1, D_RNN), F32),
    ]
    args = [x, g1, w_in, rope, rope, cw, cb, wg, bga, bgx, lam]
    aliases = {}
    aliased = k_last is not None
    if aliased:
        aliases = {len(args): 4, len(args) + 1: 5}
        in_specs += [pl.BlockSpec(memory_space=pl.ANY), pl.BlockSpec(memory_space=pl.ANY)]
        args += [k_last, v_last]
    return pl.pallas_call(
        functools.partial(_front_kernel, tm=tm, aliased=aliased),
        out_shape=out_shape,
        grid=(nt,),
        in_specs=in_specs,
        out_specs=out_specs,
        scratch_shapes=scratch,
        input_output_aliases=aliases,
        compiler_params=pltpu.CompilerParams(
            dimension_semantics=("arbitrary",), vmem_limit_bytes=VMEM_LIMIT),
        name="prompt_front",
    )(*args)


def _band_bias():
    a = np.arange(BAND)[:, None]
    c = np.arange(2 * BAND)[None, :]
    dist = BAND + a - c
    ok = (dist >= 0) & (dist <= BAND)
    with_prev = np.where(ok, 0.0, NEG_BIG)
    no_prev = np.where(ok & (c >= BAND), 0.0, NEG_BIG)
    return np.tile(np.stack([with_prev, no_prev]), (1, 2, 1)).astype(np.float32)


def _merge_rows(hp, rows, o_new, m_new, l_new, acc_s, m_s, l_s):
    m_old = m_s[hp, rows, :]
    m_tot = jnp.maximum(m_old, m_new)
    w_old = jnp.exp(m_old - m_tot)
    w_new = jnp.exp(m_new - m_tot)
    m_s[hp, rows, :] = m_tot
    l_s[hp, rows, :] = l_s[hp, rows, :] * w_old + l_new * w_new
    acc_s[hp, rows, :] = acc_s[hp, rows, :] * w_old + o_new * w_new


def _attn_kernel(bias_ref, *refs):
    pat_refs = [refs[3 * g:3 * g + 3] for g in range(3)]
    o_ref = refs[9]
    acc_s, m_s, l_s, tmp_s, nat_s = refs[10:15]
    blk = pl.program_id(0)
    step = pl.program_id(1)
    quarter = ATT_BLOCK // 4
    piece = BAND // 4

    @pl.when(step == 0)
    def _():
        acc_s[...] = jnp.zeros_like(acc_s)
        l_s[...] = jnp.zeros_like(l_s)
        m_s[...] = jnp.full(m_s.shape, NEG_BIG, F32)

    lane = lax.broadcasted_iota(jnp.int32, (BAND, LANES), 1)
    low = lane < HEAD_DIM

    for sub in range(ATT_SUB):
        s = step * ATT_SUB + sub
        first = (blk * ATT_STEPS + s == 0, blk * 4 + s // 4 == 0, blk == 0)
        base4 = (s % 4) * quarter
        rows = (None,
                pl.ds(pl.multiple_of(base4 + (s // 4) * BAND, BAND), BAND),
                pl.ds(base4 + s // 4, BAND, stride=4))
        for g in range(3):
            q_ref, prev_ref, cur_ref = pat_refs[g]
            bias = bias_ref[jnp.where(first[g], 1, 0)]
            for hp in range(N_SLABS):
                kcols = slice(hp * LANES, (hp + 1) * LANES)
                vcols = slice(D_ATT + hp * LANES, D_ATT + (hp + 1) * LANES)
                if g == 0:
                    here = slice(sub * BAND, (sub + 1) * BAND)
                    before = slice((sub - 1) * BAND, sub * BAND)
                    q2 = q_ref[here, kcols]
                    kp, vp = ((prev_ref[:, kcols], prev_ref[:, vcols]) if sub == 0 else
                              (cur_ref[before, kcols], cur_ref[before, vcols]))
                    kc, vc = cur_ref[here, kcols], cur_ref[here, vcols]
                else:
                    q2 = q_ref[sub, :, kcols]
                    kp, vp = prev_ref[sub, :, kcols], prev_ref[sub, :, vcols]
                    kc, vc = cur_ref[sub, :, kcols], cur_ref[sub, :, vcols]
                k2 = jnp.concatenate([kp, kc], axis=0)
                v2 = jnp.concatenate([vp, vc], axis=0)
                zero = jnp.zeros_like(q2)
                qq = jnp.concatenate([jnp.where(low, q2, zero), jnp.where(low, zero, q2)], axis=0)
                sc = _dot_nt(qq, k2) + bias
                mx = jnp.max(sc, axis=1, keepdims=True)
                p = jnp.exp(sc - mx)
                sm = jnp.sum(p, axis=1, keepdims=True)
                oo = _dot(p.astype(BF16), v2)
                o_new = jnp.where(low, oo[:BAND], oo[BAND:])
                m_new = jnp.where(low, mx[:BAND], mx[BAND:])
                l_new = jnp.where(low, sm[:BAND], sm[BAND:])
                if g == 0:
                    slot = 3 * (sub * N_SLABS + hp)
                    for n, val in enumerate((o_new, m_new, l_new)):
                        tmp_s[slot + n] = val
                    for c4 in range(4):
                        pick = pl.ds(c4, piece, stride=4)
                        dst = pl.ds(pl.multiple_of(c4 * quarter + s * piece, piece), piece)
                        _merge_rows(hp, dst, tmp_s[slot, pick, :], tmp_s[slot + 1, pick, :],
                                    tmp_s[slot + 2, pick, :], acc_s, m_s, l_s)
                else:
                    _merge_rows(hp, rows[g], o_new, m_new, l_new, acc_s, m_s, l_s)

    @pl.when(step == pl.num_programs(1) - 1)
    def _():
        for hp in range(N_SLABS):
            for c4 in range(4):
                part = slice(c4 * quarter, (c4 + 1) * quarter)
                nat_s[pl.ds(c4, quarter, stride=4), :] = acc_s[hp, part, :] / l_s[hp, part, :]
            o_ref[:, hp * LANES:(hp + 1) * LANES] = nat_s[...].astype(BF16)


def _attn_call(qkv1, qkv4, qkv16):
    s = qkv1.shape[0]
    nb = s // ATT_BLOCK
    steps = ATT_STEPS // ATT_SUB
    per4 = 4 // ATT_SUB
    wide = 3 * D_ATT
    q1 = qkv1.reshape(s // BAND, BAND, wide)
    q1_sub = qkv1.reshape(s // (ATT_SUB * BAND), ATT_SUB * BAND, wide)

    def width(is_q):
        return D_ATT if is_q else 2 * D_ATT

    def col(is_q):
        return QKV_SLOT[0] if is_q else 0

    def spec1(is_q, prev):
        if prev:
            return pl.BlockSpec(
                (None, BAND, width(is_q)),
                lambda b, t: (jnp.maximum((b * steps + t) * ATT_SUB - 1, 0), 0, col(is_q)))
        return pl.BlockSpec((None, ATT_SUB * BAND, width(is_q)),
                            lambda b, t: (b * steps + t, 0, col(is_q)))

    def spec4(is_q, prev):
        def im(b, t):
            j = b * 4 + t // per4
            return (jnp.maximum(j - 1, 0) if prev else j, t % per4, 0, col(is_q))
        return pl.BlockSpec((None, ATT_SUB, BAND, width(is_q)), im)

    def spec16(is_q, prev):
        def im(b, t):
            return (jnp.maximum(b - 1, 0) if prev else b, t, 0, col(is_q))
        return pl.BlockSpec((None, ATT_SUB, BAND, width(is_q)), im)

    in_specs = [pl.BlockSpec((2, 2 * BAND, 2 * BAND), lambda b, t: (0, 0, 0))]
    args = [jnp.asarray(_band_bias())]
    in_specs += [spec1(True, False), spec1(False, True), spec1(False, False)]
    args += [q1_sub, q1, q1_sub]
    for arr, mk in ((qkv4, spec4), (qkv16, spec16)):
        in_specs += [mk(True, False), mk(False, True), mk(False, False)]
        args += [arr] * 3
    state = pltpu.VMEM((N_SLABS, ATT_BLOCK, LANES), F32)
    return pl.pallas_call(
        _attn_kernel,
        out_shape=jax.ShapeDtypeStruct((s, D_ATT), BF16),
        grid=(nb, steps),
        in_specs=in_specs,
        out_specs=pl.BlockSpec((ATT_BLOCK, D_ATT), lambda b, t: (b, 0)),
        scratch_shapes=[state, state, state,
                        pltpu.VMEM((3 * ATT_SUB * N_SLABS, BAND, LANES), F32),
                        pltpu.VMEM((ATT_BLOCK, LANES), F32)],
        compiler_params=pltpu.CompilerParams(
            dimension_semantics=("arbitrary", "arbitrary"), vmem_limit_bytes=VMEM_LIMIT),
        name="prompt_attention",
    )(*args)


def _ffn_kernel(x_ref, att_ref, rnn_ref, wo_ref, g2_ref, wg_ref, wu_ref, wd_ref, gf_ref,
                *refs, final_norm, slide_nt, aliased):
    if slide_nt:
        kc_ref, vc_ref = refs[:2]
        o_ref, ko_ref, vo_ref, act_s = refs[(4 if aliased else 2):]
        w_buf = kc_ref.shape[-1]
        for h in range(kc_ref.shape[0]):
            ko_ref[h] = pltpu.roll(kc_ref[h], w_buf - slide_nt, 1)
            vo_ref[h] = pltpu.roll(vc_ref[h], w_buf - slide_nt, 1)
    else:
        o_ref, act_s = refs
    y = (x_ref[...] + _dot(att_ref[...], wo_ref[0:D_ATT, :])
         + _dot(rnn_ref[...], wo_ref[D_ATT:D_MODEL, :]))
    hf = _rmsnorm(y, g2_ref[...]).astype(BF16)
    for c in range(D_FF // FF_CHUNK):
        cols = slice(c * FF_CHUNK, (c + 1) * FF_CHUNK)
        gate = _dot(hf, wg_ref[:, cols])
        up = _dot(hf, wu_ref[:, cols])
        act_s[:, cols] = (gate * jax.nn.sigmoid(gate) * up).astype(BF16)
    out = y + _dot(act_s[...], wd_ref[...])
    if final_norm:
        out = _rmsnorm(out, gf_ref[...])
    o_ref[...] = out


def _ffn_call(x, att, rnn, wo, g2, wg, wu, wd, gf, *, layer, final_norm, tm, slide=None):
    s = x.shape[0]
    steps = s // tm
    row = lambda i: (i, 0)
    const = lambda i: (0, 0)
    pick = lambda i: (layer, 0, 0)
    once = pl.Buffered(1)
    in_specs = [
        pl.BlockSpec((tm, D_MODEL), row),
        pl.BlockSpec((tm, D_ATT), row),
        pl.BlockSpec((tm, D_RNN), row),
        pl.BlockSpec((None, D_MODEL, D_MODEL), pick, pipeline_mode=once),
        pl.BlockSpec((1, D_MODEL), const),
        pl.BlockSpec((None, D_MODEL, D_FF), pick, pipeline_mode=once),
        pl.BlockSpec((None, D_MODEL, D_FF), pick, pipeline_mode=once),
        pl.BlockSpec((None, D_FF, D_MODEL), pick, pipeline_mode=once),
        pl.BlockSpec((1, D_MODEL), const),
    ]
    args = [x, att, rnn, wo, g2, wg, wu, wd, gf]
    out_shape = jax.ShapeDtypeStruct((s, D_MODEL), F32)
    out_specs = pl.BlockSpec((tm, D_MODEL), row)
    aliases = {}
    aliased = False
    if slide is not None:
        kc, vc, k_all, v_all, nt = slide
        shape5 = kc.shape
        depth, n_batch, n_heads, _, w_buf = shape5
        units = n_batch * n_heads
        hb = units // steps
        assert hb * steps == units
        flat = (depth, units, HEAD_DIM, w_buf)
        blk_c = pl.BlockSpec((None, hb, HEAD_DIM, w_buf), lambda i: (layer, i, 0, 0))
        in_specs += [blk_c, blk_c]
        args += [kc.reshape(flat), vc.reshape(flat)]
        aliased = k_all is not None
        if aliased:
            aliases = {len(args): 1, len(args) + 1: 2}
            in_specs += [pl.BlockSpec(memory_space=pl.ANY), pl.BlockSpec(memory_space=pl.ANY)]
            args += [k_all.reshape(flat), v_all.reshape(flat)]
        full = jax.ShapeDtypeStruct(flat, F32)
        out_shape = (out_shape, full, full)
        out_specs = (out_specs, blk_c, blk_c)
    res = pl.pallas_call(
        functools.partial(_ffn_kernel, final_norm=final_norm,
                          slide_nt=slide[4] if slide is not None else 0, aliased=aliased),
        out_shape=out_shape,
        grid=(steps,),
        in_specs=in_specs,
        out_specs=out_specs,
        input_output_aliases=aliases,
        scratch_shapes=[pltpu.VMEM((tm, D_FF), BF16)],
        compiler_params=pltpu.CompilerParams(
            dimension_semantics=("arbitrary",), vmem_limit_bytes=VMEM_LIMIT),
        name="out_proj_ffn",
    )(*args)
    if slide is None:
        return res
    out, k_all, v_all = res
    return out, k_all.reshape(shape5), v_all.reshape(shape5)


def _sample_front_kernel(x_ref, g1_ref, w_ref, cos_ref, sin_ref, cw_ref, cb_ref, wg_ref,
                         bga_ref, bgx_ref, lam_ref, cst_ref, h0_ref,
                         q_ref, k_ref, v_ref, rnn_ref, nconv_ref, hl_ref, *, nb, nt):
    hn = _rmsnorm(x_ref[...], g1_ref[...]).astype(BF16)
    cos = cos_ref[...]
    sin = sin_ref[...]
    for part, out in enumerate((q_ref, k_ref, v_ref)):
        z = _dot(hn, w_ref[:, part * D_ATT:(part + 1) * D_ATT])
        for c in range(N_SLABS):
            zc = z[:, c * LANES:(c + 1) * LANES]
            if part < 2:
                zc = _rope_slab(zc, cos, sin)
            if part == 0:
                zc = zc * (1.0 / math.sqrt(HEAD_DIM))
            out[:, c * LANES:(c + 1) * LANES] = zc
    xr = _dot(hn, w_ref[:, 3 * D_ATT:3 * D_ATT + D_RNN])
    yg = _dot(hn, w_ref[:, 3 * D_ATT + D_RNN:])
    xpad = [cst_ref[j] for j in range(CONV_WIDTH - 1)]
    xpad += [xr[t * nb:(t + 1) * nb, :] for t in range(nt)]
    xc = []
    for t in range(nt):
        acc = cb_ref[...] + xpad[t] * cw_ref[0:1, :]
        for j in range(1, CONV_WIDTH):
            acc = acc + xpad[t + j] * cw_ref[j:j + 1, :]
        xc.append(acc)
    for j in range(CONV_WIDTH - 1):
        nconv_ref[j] = xpad[nt + j]
    a, b = _lru_coeffs(jnp.concatenate(xc, axis=0), wg_ref, bga_ref[...], bgx_ref[...],
                       lam_ref[...])
    h = h0_ref[...]
    hs = []
    for t in range(nt):
        h = a[t * nb:(t + 1) * nb, :] * h + b[t * nb:(t + 1) * nb, :]
        hs.append(h)
    hl_ref[...] = h
    rnn_ref[...] = (jnp.concatenate(hs, axis=0) * _gelu_tanh(yg)).astype(BF16)


def _sample_front_call(x, g1, w_in, rope, cw, cb, wg, bga, bgx, lam, cst, h0, *, layer, nb, nt):
    m = x.shape[0]

    def whole(a):
        return pl.BlockSpec(a.shape, lambda i, nd=a.ndim: (0,) * nd)

    in_specs = [whole(x), whole(g1),
                pl.BlockSpec((None, D_MODEL, D_IN), lambda i: (layer, 0, 0)),
                pl.BlockSpec((m, LANES), lambda i: (0, 0)), pl.BlockSpec((m, LANES), lambda i: (0, 1))]
    in_specs += [whole(a) for a in (cw, cb, wg, bga, bgx, lam, cst, h0)]
    out_shape = (
        jax.ShapeDtypeStruct((m, D_ATT), F32),
        jax.ShapeDtypeStruct((m, D_ATT), F32),
        jax.ShapeDtypeStruct((m, D_ATT), F32),
        jax.ShapeDtypeStruct((m, D_RNN), BF16),
        jax.ShapeDtypeStruct((CONV_WIDTH - 1, nb, D_RNN), F32),
        jax.ShapeDtypeStruct((nb, D_RNN), F32),
    )
    out_specs = tuple(pl.BlockSpec(o.shape, lambda i, nd=len(o.shape): (0,) * nd) for o in out_shape)
    return pl.pallas_call(
        functools.partial(_sample_front_kernel, nb=nb, nt=nt),
        out_shape=out_shape,
        grid=(1,),
        in_specs=in_specs,
        out_specs=out_specs,
        compiler_params=pltpu.CompilerParams(
            dimension_semantics=("arbitrary",), vmem_limit_bytes=VMEM_LIMIT),
        name="sample_front",
    )(x, g1, w_in, rope, rope, cw, cb, wg, bga, bgx, lam, cst, h0)


def _sample_mult(nt, w_buf):
    cache = np.zeros((SUBLANES, w_buf), np.float32)
    new = np.zeros((SUBLANES, LANES), np.float32)
    for dil in DILATIONS:
        for m in range(BAND + 1):
            idx = w_buf + np.arange(nt) - dil * m
            for t in range(nt):
                if idx[t] < 0:
                    continue
                if idx[t] < w_buf:
                    cache[t, idx[t]] += 1.0
                else:
                    new[t, LANES - nt + (idx[t] - w_buf)] += 1.0
    cache[nt:] = 1.0
    return cache, new


def _sample_attn_kernel(q_ref, kc_ref, vc_ref, kn_ref, vn_ref, mc_ref, mn_ref, k_any, v_any,
                        ko_ref, vo_ref, att_ref, *, nt):
    del k_any, v_any
    w_buf = kc_ref.shape[-1]
    mc = mc_ref[...]
    mn = mn_ref[...]
    lane = lax.broadcasted_iota(jnp.int32, (HEAD_DIM, LANES), 1)
    fresh = lane >= LANES - nt

    def tail_tiles(new_ref, pair):
        rows = jnp.concatenate([new_ref[0, :, pair * LANES:(pair + 1) * LANES],
                                jnp.zeros((LANES - SUBLANES, LANES), F32)], axis=0)
        placed = pltpu.roll(rows.T, LANES - nt, 1)
        return placed[:HEAD_DIM], placed[HEAD_DIM:]

    kn_tiles, vn_tiles = [], []
    for pair in range(CACHE_HEADS // 2):
        kn_tiles += tail_tiles(kn_ref, pair)
        vn_tiles += tail_tiles(vn_ref, pair)
    for h in range(CACHE_HEADS):
        q = q_ref[0, h].astype(BF16)
        kc = kc_ref[0, 0, h]
        vc = vc_ref[0, 0, h]
        kn = kn_tiles[h]
        vn = vn_tiles[h]
        sc = jnp.where(mc > 0, _dot(q, kc.astype(BF16)), NEG_BIG)
        sn = jnp.where(mn > 0, _dot(q, kn.astype(BF16)), NEG_BIG)
        mx = jnp.maximum(jnp.max(sc, axis=1, keepdims=True), jnp.max(sn, axis=1, keepdims=True))
        pc = mc * jnp.exp(sc - mx)
        pn = mn * jnp.exp(sn - mx)
        den = jnp.sum(pc, axis=1, keepdims=True) + jnp.sum(pn, axis=1, keepdims=True)
        o = _dot_nt(pc.astype(BF16), vc.astype(BF16)) + _dot_nt(pn.astype(BF16), vn.astype(BF16))
        att_ref[0, h] = o / den
        for src, new, dst in ((kc, kn, ko_ref), (vc, vn, vo_ref)):
            moved = pltpu.roll(src[:, w_buf - 2 * LANES:], 2 * LANES - nt, 1)[:, LANES:]
            dst[0, 0, h] = jnp.where(fresh, new, moved)


def _sample_attn_call(q, kc, vc, kn, vn, k_all, v_all, *, layer, nt):
    _, nb, nh, _, w_buf = kc.shape
    hb = CACHE_HEADS
    mc, mn = _sample_mult(nt, w_buf)
    blk_c = pl.BlockSpec((1, 1, hb, HEAD_DIM, w_buf), lambda b, h: (layer, b, h, 0, 0))
    blk_n = pl.BlockSpec((1, SUBLANES, hb * HEAD_DIM), lambda b, h: (b, 0, h))
    blk_q = pl.BlockSpec((1, hb, SUBLANES, HEAD_DIM), lambda b, h: (b, h, 0, 0))
    blk_o = pl.BlockSpec((1, 1, hb, HEAD_DIM, LANES), lambda b, h: (layer, b, h, 0, w_buf // LANES - 1))
    const = lambda b, h: (0, 0)
    in_specs = [blk_q, blk_c, blk_c, blk_n, blk_n,
                pl.BlockSpec((SUBLANES, w_buf), const), pl.BlockSpec((SUBLANES, LANES), const),
                pl.BlockSpec(memory_space=pl.ANY), pl.BlockSpec(memory_space=pl.ANY)]
    args = [q, kc, vc, kn, vn, jnp.asarray(mc), jnp.asarray(mn), k_all, v_all]
    aliases = {7: 0, 8: 1}
    full = jax.ShapeDtypeStruct(kc.shape, F32)
    return pl.pallas_call(
        functools.partial(_sample_attn_kernel, nt=nt),
        out_shape=(full, full, jax.ShapeDtypeStruct((nb, nh, SUBLANES, HEAD_DIM), F32)),
        grid=(nb, nh // hb),
        in_specs=in_specs,
        out_specs=(blk_o, blk_o, blk_q),
        input_output_aliases=aliases,
        compiler_params=pltpu.CompilerParams(
            dimension_semantics=("arbitrary", "arbitrary"), vmem_limit_bytes=VMEM_LIMIT),
        name="sample_attention",
    )(*args)


def _rope_tables(pos):
    half = ROT_DIM // 2
    inv = ROPE_THETA ** (-np.arange(half, dtype=np.float64) * 2.0 / ROT_DIM)
    ang = np.asarray(pos, np.float64)[:, None] * inv[None, :]
    cos, sin = np.cos(ang), np.sin(ang)
    n = ang.shape[0]
    rest = HEAD_DIM - ROT_DIM
    c = np.concatenate([cos, cos, np.ones((n, rest))], axis=1)
    t = np.concatenate([-sin, sin, np.zeros((n, rest))], axis=1)
    rep = LANES // HEAD_DIM
    return jnp.asarray(np.concatenate([np.tile(c, (1, rep)), np.tile(t, (1, rep))], axis=1), F32)


def _gate_weights(w_a, w_x):
    def dense(w):
        return jax.scipy.linalg.block_diag(*[w[n] for n in range(N_RNN_BLOCKS)])
    da, dx = dense(w_a), dense(w_x)
    half = D_RNN // 2
    chunks = [jnp.concatenate([da[c * half:(c + 1) * half, c * half:(c + 1) * half],
                               dx[c * half:(c + 1) * half, c * half:(c + 1) * half]], axis=1)
              for c in range(2)]
    return jnp.stack(chunks).astype(BF16)


def kernel(x_prompt, x_sample, cache_k, cache_v, state_conv, state_h, norm1_g, w_in, conv_w, conv_b,
           w_gate_a, b_gate_a, w_gate_x, b_gate_x, lru_lambda, w_out, norm2_g, w_ffn_gate, w_ffn_up,
           w_ffn_down, final_norm_g):
    bp, s, _ = x_prompt.shape
    nb, nt, _ = x_sample.shape
    depth = norm1_g.shape[0]
    w_buf = cache_k.shape[2]
    assert bp == 1 and s % ATT_BLOCK == 0 and w_buf == MAX_WINDOW and nt * nb == BAND

    xp = x_prompt.reshape(s, D_MODEL)
    xs = x_sample.transpose(1, 0, 2).reshape(nt * nb, D_MODEL)
    rope_p = _rope_tables(_class_major_positions(s, TM_FRONT))
    rope_s = _rope_tables(PAST_LEN + np.repeat(np.arange(nt), nb))
    ck_t = cache_k.transpose(0, 1, 3, 4, 2)
    cv_t = cache_v.transpose(0, 1, 3, 4, 2)
    cst = state_conv.transpose(0, 2, 1, 3)
    gf = final_norm_g.reshape(1, D_MODEL)
    row = lambda v: v.reshape(1, -1)

    pc, ph, sc, sh = [], [], [], []
    k_all = v_all = k_last = v_last = None
    w_in_b = w_in.astype(BF16)
    wo_b = w_out.astype(BF16)
    wg_b = w_ffn_gate.astype(BF16)
    wu_b = w_ffn_up.astype(BF16)
    wd_b = w_ffn_down.astype(BF16)
    for l in range(depth):
        gates = _gate_weights(w_gate_a[l], w_gate_x[l])
        common = (conv_w[l], row(conv_b[l]), gates, row(b_gate_a[l]), row(b_gate_x[l]),
                  row(lru_lambda[l]))
        final = l == depth - 1

        qkv1, qkv4, qkv16, rnn, k_last, v_last, tail, h_last = _front_call(
            xp, row(norm1_g[l]), w_in_b, rope_p, *common, k_last, v_last, layer=l)
        att = _attn_call(qkv1, qkv4, qkv16)
        xp, k_all, v_all = _ffn_call(xp, att, rnn, wo_b, row(norm2_g[l]), wg_b, wu_b, wd_b, gf,
                                     layer=l, final_norm=final, tm=TM_FFN,
                                     slide=(ck_t, cv_t, k_all, v_all, nt))
        pc.append(tail[:, SUBLANES - 1, :].reshape(1, CONV_WIDTH - 1, D_RNN))
        ph.append(h_last[0:1])

        q_s, k_s, v_s, rnn_s, nconv, hl = _sample_front_call(
            xs, row(norm1_g[l]), w_in_b, rope_s, *common, cst[l], state_h[l],
            layer=l, nb=nb, nt=nt)

        def split(v):
            return v.reshape(nt, nb, N_HEADS, HEAD_DIM)

        q_b = jnp.pad(split(q_s).transpose(1, 2, 0, 3), ((0, 0), (0, 0), (0, SUBLANES - nt), (0, 0)))

        def fresh(v):
            return jnp.pad(v.reshape(nt, nb, D_ATT).transpose(1, 0, 2),
                           ((0, 0), (0, SUBLANES - nt), (0, 0)))

        k_all, v_all, att_s = _sample_attn_call(
            q_b, ck_t, cv_t, fresh(k_s), fresh(v_s), k_all, v_all, layer=l, nt=nt)
        att_s = att_s[:, :, :nt].transpose(2, 0, 1, 3).reshape(nt * nb, D_ATT).astype(BF16)
        xs = _ffn_call(xs, att_s, rnn_s, wo_b, row(norm2_g[l]), wg_b, wu_b, wd_b, gf,
                       layer=l, final_norm=final, tm=nt * nb)
        sc.append(nconv.transpose(1, 0, 2))
        sh.append(hl)

    y_prompt = xp.reshape(1, s, D_MODEL)
    y_sample = xs.reshape(nt, nb, D_MODEL).transpose(1, 0, 2)
    sample_k = k_all.transpose(0, 1, 4, 2, 3)
    sample_v = v_all.transpose(0, 1, 4, 2, 3)
    window = (depth, 1, N_HEADS, HEAD_DIM, MAX_WINDOW)
    prompt_k = k_last.reshape(window).transpose(0, 1, 4, 2, 3)
    prompt_v = v_last.reshape(window).transpose(0, 1, 4, 2, 3)
    return (y_prompt, y_sample, prompt_k, prompt_v, jnp.stack(pc), jnp.stack(ph),
            sample_k, sample_v, jnp.stack(sc), jnp.stack(sh))
```

```python
import functools
import math

import numpy as np
import jax
import jax.numpy as jnp
from jax import lax
from jax.experimental import pallas as pl
from jax.experimental.pallas import tpu as pltpu

F32 = jnp.float32
BF16 = jnp.bfloat16

D_MODEL = 1024
HEAD_DIM = 64
D_ATT = 512
N_HEADS = 8
D_RNN = 512
N_RNN_BLOCKS = 8
CONV_WIDTH = 4
LRU_C = 8.0
DILATIONS = (1, 4, 16)
BAND = 128
MAX_WINDOW = 2048
ROPE_THETA = 500000.0
ROT_DIM = 16
D_FF = 2816
D_IN = 3 * D_ATT + 2 * D_RNN
RMS_EPS = 1e-6
PAST_LEN = 16384

LANES = 128
SUBLANES = 8
N_SLABS = D_ATT // LANES
FF_CHUNK = 256
NEG_BIG = -1e30
VMEM_LIMIT = 56 * 1024 * 1024

TM_FRONT = 512
TM_FFN = 256
ATT_BLOCK = MAX_WINDOW
ATT_STEPS = ATT_BLOCK // BAND
ATT_SUB = 4
CACHE_HEADS = 8
QKV_SLOT = (2, 0, 1)


def _dot(a, b):
    return jnp.dot(a, b, preferred_element_type=F32)


def _dot_nt(a, b):
    return lax.dot_general(a, b, (((1,), (1,)), ((), ())), preferred_element_type=F32)


def _rmsnorm(x, g):
    return x * lax.rsqrt(jnp.mean(x * x, axis=-1, keepdims=True) + RMS_EPS) * g


def _softplus(x):
    return jnp.maximum(x, 0.0) + jnp.log1p(jnp.exp(-jnp.abs(x)))


def _gelu_tanh(x):
    return 0.5 * x * (1.0 + jnp.tanh(math.sqrt(2.0 / math.pi) * (x + 0.044715 * (x * x * x))))


def _rope_slab(x, c, t):
    half = ROT_DIM // 2
    lane = lax.broadcasted_iota(jnp.int32, x.shape, 1)
    partner = jnp.where((lane & half) == 0, pltpu.roll(x, LANES - half, 1), pltpu.roll(x, half, 1))
    return x * c + partner * t


def _lru_coeffs(xc, wg_ref, bga, bgx, lam):
    xb = xc.astype(BF16)
    half = D_RNN // 2
    g0 = _dot(xb[:, :half], wg_ref[0])
    g1 = _dot(xb[:, half:], wg_ref[1])
    r = jax.nn.sigmoid(jnp.concatenate([g0[:, :half], g1[:, :half]], axis=1) + bga)
    ig = jax.nn.sigmoid(jnp.concatenate([g0[:, half:], g1[:, half:]], axis=1) + bgx)
    log_a = (-LRU_C) * r * _softplus(-lam)
    a = jnp.exp(log_a)
    u = -jnp.tanh(log_a)
    twice = 2.0 * u
    root = jnp.where(u > 0.0, twice * lax.rsqrt(twice * (1.0 + u)), 0.0)
    b = root * (ig * xc)
    return a, b


def _shift_rows(x, k, fill):
    row = lax.broadcasted_iota(jnp.int32, x.shape, 0)
    return jnp.where(row >= k, pltpu.roll(x, k, 0), fill)


def _front_kernel(x_ref, g1_ref, w_ref, cos_ref, sin_ref, cw_ref, cb_ref, wg_ref,
                  bga_ref, bgx_ref, lam_ref,
                  *refs, tm, aliased):
    if aliased:
        refs = refs[2:]
    qkv1_ref, qkv4_ref, qkv16_ref, rnn_ref, klast_ref, vlast_ref, tail_ref, hlast_ref = refs[:8]
    xs_s, hn_s, cls_s, nat_s, rn_s, carry_s, h_s = refs[8:]
    i = pl.program_id(0)
    grp = tm // 4
    n16 = tm // 16

    @pl.when(i == 0)
    def _():
        carry_s[...] = jnp.zeros_like(carry_s)
        h_s[...] = jnp.zeros_like(h_s)

    hn32 = _rmsnorm(x_ref[...], g1_ref[...])
    n_x = D_MODEL // LANES
    for c in range(n_x):
        xs_s[c] = hn32[:, c * LANES:(c + 1) * LANES]
    for j in range(4):
        for c in range(n_x):
            hn_s[j * grp:(j + 1) * grp, c * LANES:(c + 1) * LANES] = (
                xs_s[c, pl.ds(j, grp, stride=4), :].astype(BF16))

    xr = _dot(hn_s[...], w_ref[:, 3 * D_ATT:3 * D_ATT + D_RNN])
    yg = _dot(hn_s[...], w_ref[:, 3 * D_ATT + D_RNN:])
    slab = [xr[j * grp:(j + 1) * grp, :] for j in range(4)]
    back = [None] + [_shift_rows(slab[j], 1, carry_s[j - 1, SUBLANES - 1:SUBLANES, :])
                     for j in range(1, 4)]
    for j in range(1, 4):
        carry_s[j - 1] = slab[j][grp - SUBLANES:, :]
        tail_ref[j - 1] = slab[j][grp - SUBLANES:, :]
    w = [cw_ref[k:k + 1, :] for k in range(CONV_WIDTH)]
    taps = ((slab[0], back[3], back[2], back[1]),
            (slab[1], slab[0], back[3], back[2]),
            (slab[2], slab[1], slab[0], back[3]),
            (slab[3], slab[2], slab[1], slab[0]))
    xc = jnp.concatenate(
        [cb_ref[...] + t[0] * w[3] + t[1] * w[2] + t[2] * w[1] + t[3] * w[0] for t in taps], axis=0)
    a, b = _lru_coeffs(xc, wg_ref, bga_ref[...], bgx_ref[...], lam_ref[...])

    pp = [a[0:grp, :]]
    hh = [b[0:grp, :]]
    for j in range(1, 4):
        aj = a[j * grp:(j + 1) * grp, :]
        hh.append(aj * hh[j - 1] + b[j * grp:(j + 1) * grp, :])
        pp.append(aj * pp[j - 1])
    pc, hc = pp[3], hh[3]
    row = lax.broadcasted_iota(jnp.int32, (grp, D_RNN), 0)
    k = 1
    while k < grp:
        keep = row >= k
        hc = jnp.where(keep, hc + pc * pltpu.roll(hc, k, 0), hc)
        pc = jnp.where(keep, pc * pltpu.roll(pc, k, 0), pc)
        k *= 2
    h_in = h_s[...]
    ends = hc + pc * h_in
    h_prev = _shift_rows(ends, 1, h_in)
    h_s[...] = ends[grp - 1:grp, :]
    hlast_ref[...] = jnp.broadcast_to(ends[grp - 1:grp, :], (SUBLANES, D_RNN))
    for j in range(4):
        gated = (hh[j] + pp[j] * h_prev) * _gelu_tanh(yg[j * grp:(j + 1) * grp, :])
        for c in range(N_SLABS):
            rn_s[c, pl.ds(j, grp, stride=4), :] = gated[:, c * LANES:(c + 1) * LANES]
    for c in range(N_SLABS):
        rnn_ref[:, c * LANES:(c + 1) * LANES] = rn_s[c].astype(BF16)

    cos = cos_ref[...]
    sin = sin_ref[...]
    for part in range(3):
        z = _dot(hn_s[...], w_ref[:, part * D_ATT:(part + 1) * D_ATT])
        for c in range(N_SLABS):
            zc = z[:, c * LANES:(c + 1) * LANES]
            if part < 2:
                zc = _rope_slab(zc, cos, sin)
            if part == 0:
                zc = zc * (1.0 / math.sqrt(HEAD_DIM))
            n = QKV_SLOT[part] * N_SLABS + c
            cols = slice(n * LANES, (n + 1) * LANES)
            cls_s[n] = zc
            for j in range(4):
                rows = zc[j * grp:(j + 1) * grp, :]
                qkv4_ref[0, j, :, cols] = rows.astype(BF16)
                nat_s[n, pl.ds(j, grp, stride=4), :] = rows
            for j in range(4):
                for jj in range(4):
                    rows = cls_s[n, pl.ds(j * grp + jj, n16, stride=4), :]
                    qkv16_ref[0, j + 4 * jj, :, cols] = rows.astype(BF16)
            qkv1_ref[:, cols] = nat_s[n].astype(BF16)

    @pl.when(i >= pl.num_programs(0) - MAX_WINDOW // tm)
    def _():
        for c in range(N_SLABS):
            rows = slice(c * LANES, (c + 1) * LANES)
            klast_ref[rows, :] = nat_s[QKV_SLOT[1] * N_SLABS + c].T
            vlast_ref[rows, :] = nat_s[QKV_SLOT[2] * N_SLABS + c].T


def _class_major_positions(s, tm):
    r = np.arange(s)
    tile, within = r // tm, r % tm
    grp = tm // 4
    return (tile * tm + 4 * (within % grp) + within // grp).astype(np.int32)


def _front_call(x, g1, w_in, rope, cw, cb, wg, bga, bgx, lam, k_last, v_last, *, layer):
    s = x.shape[0]
    depth = w_in.shape[0]
    tm = TM_FRONT
    nt = s // tm
    n_last = MAX_WINDOW // tm
    row = lambda i: (i, 0)
    const2 = lambda i: (0, 0)
    const3 = lambda i: (0, 0, 0)
    last = lambda i: (layer, 0, jnp.maximum(i - (nt - n_last), 0))
    t4 = 4 * BAND // tm
    t16 = ATT_BLOCK // tm
    out_shape = (
        jax.ShapeDtypeStruct((s, 3 * D_ATT), BF16),
        jax.ShapeDtypeStruct((s // (4 * BAND), 4, BAND, 3 * D_ATT), BF16),
        jax.ShapeDtypeStruct((s // ATT_BLOCK, ATT_STEPS, BAND, 3 * D_ATT), BF16),
        jax.ShapeDtypeStruct((s, D_RNN), BF16),
        jax.ShapeDtypeStruct((depth, D_ATT, MAX_WINDOW), F32),
        jax.ShapeDtypeStruct((depth, D_ATT, MAX_WINDOW), F32),
        jax.ShapeDtypeStruct((CONV_WIDTH - 1, SUBLANES, D_RNN), F32),
        jax.ShapeDtypeStruct((SUBLANES, D_RNN), F32),
    )
    out_specs = (
        pl.BlockSpec((tm, 3 * D_ATT), row),
        pl.BlockSpec((1, 4, tm // 4, 3 * D_ATT), lambda i: (i // t4, 0, i % t4, 0)),
        pl.BlockSpec((1, ATT_STEPS, tm // ATT_STEPS, 3 * D_ATT), lambda i: (i // t16, 0, i % t16, 0)),
        pl.BlockSpec((tm, D_RNN), row),
        pl.BlockSpec((None, D_ATT, tm), last),
        pl.BlockSpec((None, D_ATT, tm), last),
        pl.BlockSpec((CONV_WIDTH - 1, SUBLANES, D_RNN), const3),
        pl.BlockSpec((SUBLANES, D_RNN), const2),
    )
    in_specs = [
        pl.BlockSpec((tm, D_MODEL), row),
        pl.BlockSpec((1, D_MODEL), const2),
        pl.BlockSpec((None, D_MODEL, D_IN), lambda i: (layer, 0, 0), pipeline_mode=pl.Buffered(1)),
        pl.BlockSpec((tm, LANES), lambda i: (i, 0)),
        pl.BlockSpec((tm, LANES), lambda i: (i, 1)),
        pl.BlockSpec((CONV_WIDTH, D_RNN), const2),
        pl.BlockSpec((1, D_RNN), const2),
        pl.BlockSpec((2, D_RNN // 2, D_RNN), const3),
        pl.BlockSpec((1, D_RNN), const2),
        pl.BlockSpec((1, D_RNN), const2),
        pl.BlockSpec((1, D_RNN), const2),
    ]
    scratch = [
        pltpu.VMEM((D_MODEL // LANES, tm, LANES), F32),
        pltpu.VMEM((tm, D_MODEL), BF16),
        pltpu.VMEM((3 * N_SLABS, tm, LANES), F32),
        pltpu.VMEM((3 * N_SLABS, tm, LANES), F32),
        pltpu.VMEM((N_SLABS, tm, LANES), F32),
        pltpu.VMEM((CONV_WIDTH - 1, SUBLANES, D_RNN), F32),
        pltpu.VMEM((1, D_RNN), F32),
    ]
    args = [x, g1, w_in, rope, rope, cw, cb, wg, bga, bgx, lam]
    aliases = {}
    aliased = k_last is not None
    if aliased:
        aliases = {len(args): 4, len(args) + 1: 5}
        in_specs += [pl.BlockSpec(memory_space=pl.ANY), pl.BlockSpec(memory_space=pl.ANY)]
        args += [k_last, v_last]
    return pl.pallas_call(
        functools.partial(_front_kernel, tm=tm, aliased=aliased),
        out_shape=out_shape,
        grid=(nt,),
        in_specs=in_specs,
        out_specs=out_specs,
        scratch_shapes=scratch,
        input_output_aliases=aliases,
        compiler_params=pltpu.CompilerParams(
            dimension_semantics=("arbitrary",), vmem_limit_bytes=VMEM_LIMIT),
        name="prompt_front",
    )(*args)


def _band_bias():
    a = np.arange(BAND)[:, None]
    c = np.arange(2 * BAND)[None, :]
    dist = BAND + a - c
    ok = (dist >= 0) & (dist <= BAND)
    with_prev = np.where(ok, 0.0, NEG_BIG)
    no_prev = np.where(ok & (c >= BAND), 0.0, NEG_BIG)
    return np.tile(np.stack([with_prev, no_prev]), (1, 2, 1)).astype(np.float32)


def _merge_rows(hp, rows, o_new, m_new, l_new, acc_s, m_s, l_s):
    m_old = m_s[hp, rows, :]
    m_tot = jnp.maximum(m_old, m_new)
    w_old = jnp.exp(m_old - m_tot)
    w_new = jnp.exp(m_new - m_tot)
    m_s[hp, rows, :] = m_tot
    l_s[hp, rows, :] = l_s[hp, rows, :] * w_old + l_new * w_new
    acc_s[hp, rows, :] = acc_s[hp, rows, :] * w_old + o_new * w_new


def _attn_kernel(bias_ref, *refs):
    pat_refs = [refs[3 * g:3 * g + 3] for g in range(3)]
    o_ref = refs[9]
    acc_s, m_s, l_s, tmp_s, nat_s = refs[10:15]
    blk = pl.program_id(0)
    step = pl.program_id(1)
    quarter = ATT_BLOCK // 4
    piece = BAND // 4

    @pl.when(step == 0)
    def _():
        acc_s[...] = jnp.zeros_like(acc_s)
        l_s[...] = jnp.zeros_like(l_s)
        m_s[...] = jnp.full(m_s.shape, NEG_BIG, F32)

    lane = lax.broadcasted_iota(jnp.int32, (BAND, LANES), 1)
    low = lane < HEAD_DIM

    for sub in range(ATT_SUB):
        s = step * ATT_SUB + sub
        first = (blk * ATT_STEPS + s == 0, blk * 4 + s // 4 == 0, blk == 0)
        base4 = (s % 4) * quarter
        rows = (None,
                pl.ds(pl.multiple_of(base4 + (s // 4) * BAND, BAND), BAND),
                pl.ds(base4 + s // 4, BAND, stride=4))
        for g in range(3):
            q_ref, prev_ref, cur_ref = pat_refs[g]
            bias = bias_ref[jnp.where(first[g], 1, 0)]
            for hp in range(N_SLABS):
                kcols = slice(hp * LANES, (hp + 1) * LANES)
                vcols = slice(D_ATT + hp * LANES, D_ATT + (hp + 1) * LANES)
                if g == 0:
                    here = slice(sub * BAND, (sub + 1) * BAND)
                    before = slice((sub - 1) * BAND, sub * BAND)
                    q2 = q_ref[here, kcols]
                    kp, vp = ((prev_ref[:, kcols], prev_ref[:, vcols]) if sub == 0 else
                              (cur_ref[before, kcols], cur_ref[before, vcols]))
                    kc, vc = cur_ref[here, kcols], cur_ref[here, vcols]
                else:
                    q2 = q_ref[sub, :, kcols]
                    kp, vp = prev_ref[sub, :, kcols], prev_ref[sub, :, vcols]
                    kc, vc = cur_ref[sub, :, kcols], cur_ref[sub, :, vcols]
                k2 = jnp.concatenate([kp, kc], axis=0)
                v2 = jnp.concatenate([vp, vc], axis=0)
                zero = jnp.zeros_like(q2)
                qq = jnp.concatenate([jnp.where(low, q2, zero), jnp.where(low, zero, q2)], axis=0)
                sc = _dot_nt(qq, k2) + bias
                mx = jnp.max(sc, axis=1, keepdims=True)
                p = jnp.exp(sc - mx)
                sm = jnp.sum(p, axis=1, keepdims=True)
                oo = _dot(p.astype(BF16), v2)
                o_new = jnp.where(low, oo[:BAND], oo[BAND:])
                m_new = jnp.where(low, mx[:BAND], mx[BAND:])
                l_new = jnp.where(low, sm[:BAND], sm[BAND:])
                if g == 0:
                    slot = 3 * (sub * N_SLABS + hp)
                    for n, val in enumerate((o_new, m_new, l_new)):
                        tmp_s[slot + n] = val
                    for c4 in range(4):
                        pick = pl.ds(c4, piece, stride=4)
                        dst = pl.ds(pl.multiple_of(c4 * quarter + s * piece, piece), piece)
                        _merge_rows(hp, dst, tmp_s[slot, pick, :], tmp_s[slot + 1, pick, :],
                                    tmp_s[slot + 2, pick, :], acc_s, m_s, l_s)
                else:
                    _merge_rows(hp, rows[g], o_new, m_new, l_new, acc_s, m_s, l_s)

    @pl.when(step == pl.num_programs(1) - 1)
    def _():
        for hp in range(N_SLABS):
            for c4 in range(4):
                part = slice(c4 * quarter, (c4 + 1) * quarter)
                nat_s[pl.ds(c4, quarter, stride=4), :] = acc_s[hp, part, :] / l_s[hp, part, :]
            o_ref[:, hp * LANES:(hp + 1) * LANES] = nat_s[...].astype(BF16)


def _attn_call(qkv1, qkv4, qkv16):
    s = qkv1.shape[0]
    nb = s // ATT_BLOCK
    steps = ATT_STEPS // ATT_SUB
    per4 = 4 // ATT_SUB
    wide = 3 * D_ATT
    q1 = qkv1.reshape(s // BAND, BAND, wide)
    q1_sub = qkv1.reshape(s // (ATT_SUB * BAND), ATT_SUB * BAND, wide)

    def width(is_q):
        return D_ATT if is_q else 2 * D_ATT

    def col(is_q):
        return QKV_SLOT[0] if is_q else 0

    def spec1(is_q, prev):
        if prev:
            return pl.BlockSpec(
                (None, BAND, width(is_q)),
                lambda b, t: (jnp.maximum((b * steps + t) * ATT_SUB - 1, 0), 0, col(is_q)))
        return pl.BlockSpec((None, ATT_SUB * BAND, width(is_q)),
                            lambda b, t: (b * steps + t, 0, col(is_q)))

    def spec4(is_q, prev):
        def im(b, t):
            j = b * 4 + t // per4
            return (jnp.maximum(j - 1, 0) if prev else j, t % per4, 0, col(is_q))
        return pl.BlockSpec((None, ATT_SUB, BAND, width(is_q)), im)

    def spec16(is_q, prev):
        def im(b, t):
            return (jnp.maximum(b - 1, 0) if prev else b, t, 0, col(is_q))
        return pl.BlockSpec((None, ATT_SUB, BAND, width(is_q)), im)

    in_specs = [pl.BlockSpec((2, 2 * BAND, 2 * BAND), lambda b, t: (0, 0, 0))]
    args = [jnp.asarray(_band_bias())]
    in_specs += [spec1(True, False), spec1(False, True), spec1(False, False)]
    args += [q1_sub, q1, q1_sub]
    for arr, mk in ((qkv4, spec4), (qkv16, spec16)):
        in_specs += [mk(True, False), mk(False, True), mk(False, False)]
        args += [arr] * 3
    state = pltpu.VMEM((N_SLABS, ATT_BLOCK, LANES), F32)
    return pl.pallas_call(
        _attn_kernel,
        out_shape=jax.ShapeDtypeStruct((s, D_ATT), BF16),
        grid=(nb, steps),
        in_specs=in_specs,
        out_specs=pl.BlockSpec((ATT_BLOCK, D_ATT), lambda b, t: (b, 0)),
        scratch_shapes=[state, state, state,
                        pltpu.VMEM((3 * ATT_SUB * N_SLABS, BAND, LANES), F32),
                        pltpu.VMEM((ATT_BLOCK, LANES), F32)],
        compiler_params=pltpu.CompilerParams(
            dimension_semantics=("arbitrary", "arbitrary"), vmem_limit_bytes=VMEM_LIMIT),
        name="prompt_attention",
    )(*args)


def _ffn_kernel(x_ref, att_ref, rnn_ref, wo_ref, g2_ref, wg_ref, wu_ref, wd_ref, gf_ref,
                *refs, final_norm, slide_nt, aliased):
    if slide_nt:
        kc_ref, vc_ref = refs[:2]
        o_ref, ko_ref, vo_ref, act_s = refs[(4 if aliased else 2):]
        w_buf = kc_ref.shape[-1]
        for h in range(kc_ref.shape[0]):
            ko_ref[h] = pltpu.roll(kc_ref[h], w_buf - slide_nt, 1)
            vo_ref[h] = pltpu.roll(vc_ref[h], w_buf - slide_nt, 1)
    else:
        o_ref, act_s = refs
    y = (x_ref[...] + _dot(att_ref[...], wo_ref[0:D_ATT, :])
         + _dot(rnn_ref[...], wo_ref[D_ATT:D_MODEL, :]))
    hf = _rmsnorm(y, g2_ref[...]).astype(BF16)
    for c in range(D_FF // FF_CHUNK):
        cols = slice(c * FF_CHUNK, (c + 1) * FF_CHUNK)
        gate = _dot(hf, wg_ref[:, cols])
        up = _dot(hf, wu_ref[:, cols])
        act_s[:, cols] = (gate * jax.nn.sigmoid(gate) * up).astype(BF16)
    out = y + _dot(act_s[...], wd_ref[...])
    if final_norm:
        out = _rmsnorm(out, gf_ref[...])
    o_ref[...] = out


def _ffn_call(x, att, rnn, wo, g2, wg, wu, wd, gf, *, layer, final_norm, tm, slide=None):
    s = x.shape[0]
    steps = s // tm
    row = lambda i: (i, 0)
    const = lambda i: (0, 0)
    pick = lambda i: (layer, 0, 0)
    once = pl.Buffered(1)
    in_specs = [
        pl.BlockSpec((tm, D_MODEL), row),
        pl.BlockSpec((tm, D_ATT), row),
        pl.BlockSpec((tm, D_RNN), row),
        pl.BlockSpec((None, D_MODEL, D_MODEL), pick, pipeline_mode=once),
        pl.BlockSpec((1, D_MODEL), const),
        pl.BlockSpec((None, D_MODEL, D_FF), pick, pipeline_mode=once),
        pl.BlockSpec((None, D_MODEL, D_FF), pick, pipeline_mode=once),
        pl.BlockSpec((None, D_FF, D_MODEL), pick, pipeline_mode=once),
        pl.BlockSpec((1, D_MODEL), const),
    ]
    args = [x, att, rnn, wo, g2, wg, wu, wd, gf]
    out_shape = jax.ShapeDtypeStruct((s, D_MODEL), F32)
    out_specs = pl.BlockSpec((tm, D_MODEL), row)
    aliases = {}
    aliased = False
    if slide is not None:
        kc, vc, k_all, v_all, nt = slide
        shape5 = kc.shape
        depth, n_batch, n_heads, _, w_buf = shape5
        units = n_batch * n_heads
        hb = units // steps
        assert hb * steps == units
        flat = (depth, units, HEAD_DIM, w_buf)
        blk_c = pl.BlockSpec((None, hb, HEAD_DIM, w_buf), lambda i: (layer, i, 0, 0))
        in_specs += [blk_c, blk_c]
        args += [kc.reshape(flat), vc.reshape(flat)]
        aliased = k_all is not None
        if aliased:
            aliases = {len(args): 1, len(args) + 1: 2}
            in_specs += [pl.BlockSpec(memory_space=pl.ANY), pl.BlockSpec(memory_space=pl.ANY)]
            args += [k_all.reshape(flat), v_all.reshape(flat)]
        full = jax.ShapeDtypeStruct(flat, F32)
        out_shape = (out_shape, full, full)
        out_specs = (out_specs, blk_c, blk_c)
    res = pl.pallas_call(
        functools.partial(_ffn_kernel, final_norm=final_norm,
                          slide_nt=slide[4] if slide is not None else 0, aliased=aliased),
        out_shape=out_shape,
        grid=(steps,),
        in_specs=in_specs,
        out_specs=out_specs,
        input_output_aliases=aliases,
        scratch_shapes=[pltpu.VMEM((tm, D_FF), BF16)],
        compiler_params=pltpu.CompilerParams(
            dimension_semantics=("arbitrary",), vmem_limit_bytes=VMEM_LIMIT),
        name="out_proj_ffn",
    )(*args)
    if slide is None:
        return res
    out, k_all, v_all = res
    return out, k_all.reshape(shape5), v_all.reshape(shape5)


def _sample_front_kernel(x_ref, g1_ref, w_ref, cos_ref, sin_ref, cw_ref, cb_ref, wg_ref,
                         bga_ref, bgx_ref, lam_ref, cst_ref, h0_ref,
                         q_ref, k_ref, v_ref, rnn_ref, nconv_ref, hl_ref, *, nb, nt):
    hn = _rmsnorm(x_ref[...], g1_ref[...]).astype(BF16)
    cos = cos_ref[...]
    sin = sin_ref[...]
    for part, out in enumerate((q_ref, k_ref, v_ref)):
        z = _dot(hn, w_ref[:, part * D_ATT:(part + 1) * D_ATT])
        for c in range(N_SLABS):
            zc = z[:, c * LANES:(c + 1) * LANES]
            if part < 2:
                zc = _rope_slab(zc, cos, sin)
            if part == 0:
                zc = zc * (1.0 / math.sqrt(HEAD_DIM))
            out[:, c * LANES:(c + 1) * LANES] = zc
    xr = _dot(hn, w_ref[:, 3 * D_ATT:3 * D_ATT + D_RNN])
    yg = _dot(hn, w_ref[:, 3 * D_ATT + D_RNN:])
    xpad = [cst_ref[j] for j in range(CONV_WIDTH - 1)]
    xpad += [xr[t * nb:(t + 1) * nb, :] for t in range(nt)]
    xc = []
    for t in range(nt):
        acc = cb_ref[...] + xpad[t] * cw_ref[0:1, :]
        for j in range(1, CONV_WIDTH):
            acc = acc + xpad[t + j] * cw_ref[j:j + 1, :]
        xc.append(acc)
    for j in range(CONV_WIDTH - 1):
        nconv_ref[j] = xpad[nt + j]
    a, b = _lru_coeffs(jnp.concatenate(xc, axis=0), wg_ref, bga_ref[...], bgx_ref[...],
                       lam_ref[...])
    h = h0_ref[...]
    hs = []
    for t in range(nt):
        h = a[t * nb:(t + 1) * nb, :] * h + b[t * nb:(t + 1) * nb, :]
        hs.append(h)
    hl_ref[...] = h
    rnn_ref[...] = (jnp.concatenate(hs, axis=0) * _gelu_tanh(yg)).astype(BF16)


def _sample_front_call(x, g1, w_in, rope, cw, cb, wg, bga, bgx, lam, cst, h0, *, layer, nb, nt):
    m = x.shape[0]

    def whole(a):
        return pl.BlockSpec(a.shape, lambda i, nd=a.ndim: (0,) * nd)

    in_specs = [whole(x), whole(g1),
                pl.BlockSpec((None, D_MODEL, D_IN), lambda i: (layer, 0, 0)),
                pl.BlockSpec((m, LANES), lambda i: (0, 0)), pl.BlockSpec((m, LANES), lambda i: (0, 1))]
    in_specs += [whole(a) for a in (cw, cb, wg, bga, bgx, lam, cst, h0)]
    out_shape = (
        jax.ShapeDtypeStruct((m, D_ATT), F32),
        jax.ShapeDtypeStruct((m, D_ATT), F32),
        jax.ShapeDtypeStruct((m, D_ATT), F32),
        jax.ShapeDtypeStruct((m, D_RNN), BF16),
        jax.ShapeDtypeStruct((CONV_WIDTH - 1, nb, D_RNN), F32),
        jax.ShapeDtypeStruct((nb, D_RNN), F32),
    )
    out_specs = tuple(pl.BlockSpec(o.shape, lambda i, nd=len(o.shape): (0,) * nd) for o in out_shape)
    return pl.pallas_call(
        functools.partial(_sample_front_kernel, nb=nb, nt=nt),
        out_shape=out_shape,
        grid=(1,),
        in_specs=in_specs,
        out_specs=out_specs,
        compiler_params=pltpu.CompilerParams(
            dimension_semantics=("arbitrary",), vmem_limit_bytes=VMEM_LIMIT),
        name="sample_front",
    )(x, g1, w_in, rope, rope, cw, cb, wg, bga, bgx, lam, cst, h0)


def _sample_mult(nt, w_buf):
    cache = np.zeros((SUBLANES, w_buf), np.float32)
    new = np.zeros((SUBLANES, LANES), np.float32)
    for dil in DILATIONS:
        for m in range(BAND + 1):
            idx = w_buf + np.arange(nt) - dil * m
            for t in range(nt):
                if idx[t] < 0:
                    continue
                if idx[t] < w_buf:
                    cache[t, idx[t]] += 1.0
                else:
                    new[t, LANES - nt + (idx[t] - w_buf)] += 1.0
    cache[nt:] = 1.0
    return cache, new


def _sample_attn_kernel(q_ref, kc_ref, vc_ref, kn_ref, vn_ref, mc_ref, mn_ref, k_any, v_any,
                        ko_ref, vo_ref, att_ref, *, nt):
    del k_any, v_any
    w_buf = kc_ref.shape[-1]
    mc = mc_ref[...]
    mn = mn_ref[...]
    lane = lax.broadcasted_iota(jnp.int32, (HEAD_DIM, LANES), 1)
    fresh = lane >= LANES - nt

    def tail_tiles(new_ref, pair):
        rows = jnp.concatenate([new_ref[0, :, pair * LANES:(pair + 1) * LANES],
                                jnp.zeros((LANES - SUBLANES, LANES), F32)], axis=0)
        placed = pltpu.roll(rows.T, LANES - nt, 1)
        return placed[:HEAD_DIM], placed[HEAD_DIM:]

    kn_tiles, vn_tiles = [], []
    for pair in range(CACHE_HEADS // 2):
        kn_tiles += tail_tiles(kn_ref, pair)
        vn_tiles += tail_tiles(vn_ref, pair)
    for h in range(CACHE_HEADS):
        q = q_ref[0, h].astype(BF16)
        kc = kc_ref[0, 0, h]
        vc = vc_ref[0, 0, h]
        kn = kn_tiles[h]
        vn = vn_tiles[h]
        sc = jnp.where(mc > 0, _dot(q, kc.astype(BF16)), NEG_BIG)
        sn = jnp.where(mn > 0, _dot(q, kn.astype(BF16)), NEG_BIG)
        mx = jnp.maximum(jnp.max(sc, axis=1, keepdims=True), jnp.max(sn, axis=1, keepdims=True))
        pc = mc * jnp.exp(sc - mx)
        pn = mn * jnp.exp(sn - mx)
        den = jnp.sum(pc, axis=1, keepdims=True) + jnp.sum(pn, axis=1, keepdims=True)
        o = _dot_nt(pc.astype(BF16), vc.astype(BF16)) + _dot_nt(pn.astype(BF16), vn.astype(BF16))
        att_ref[0, h] = o / den
        for src, new, dst in ((kc, kn, ko_ref), (vc, vn, vo_ref)):
            moved = pltpu.roll(src[:, w_buf - 2 * LANES:], 2 * LANES - nt, 1)[:, LANES:]
            dst[0, 0, h] = jnp.where(fresh, new, moved)


def _sample_attn_call(q, kc, vc, kn, vn, k_all, v_all, *, layer, nt):
    _, nb, nh, _, w_buf = kc.shape
    hb = CACHE_HEADS
    mc, mn = _sample_mult(nt, w_buf)
    blk_c = pl.BlockSpec((1, 1, hb, HEAD_DIM, w_buf), lambda b, h: (layer, b, h, 0, 0))
    blk_n = pl.BlockSpec((1, SUBLANES, hb * HEAD_DIM), lambda b, h: (b, 0, h))
    blk_q = pl.BlockSpec((1, hb, SUBLANES, HEAD_DIM), lambda b, h: (b, h, 0, 0))
    blk_o = pl.BlockSpec((1, 1, hb, HEAD_DIM, LANES), lambda b, h: (layer, b, h, 0, w_buf // LANES - 1))
    const = lambda b, h: (0, 0)
    in_specs = [blk_q, blk_c, blk_c, blk_n, blk_n,
                pl.BlockSpec((SUBLANES, w_buf), const), pl.BlockSpec((SUBLANES, LANES), const),
                pl.BlockSpec(memory_space=pl.ANY), pl.BlockSpec(memory_space=pl.ANY)]
    args = [q, kc, vc, kn, vn, jnp.asarray(mc), jnp.asarray(mn), k_all, v_all]
    aliases = {7: 0, 8: 1}
    full = jax.ShapeDtypeStruct(kc.shape, F32)
    return pl.pallas_call(
        functools.partial(_sample_attn_kernel, nt=nt),
        out_shape=(full, full, jax.ShapeDtypeStruct((nb, nh, SUBLANES, HEAD_DIM), F32)),
        grid=(nb, nh // hb),
        in_specs=in_specs,
        out_specs=(blk_o, blk_o, blk_q),
        input_output_aliases=aliases,
        compiler_params=pltpu.CompilerParams(
            dimension_semantics=("arbitrary", "arbitrary"), vmem_limit_bytes=VMEM_LIMIT),
        name="sample_attention",
    )(*args)


def _rope_tables(pos):
    half = ROT_DIM // 2
    inv = ROPE_THETA ** (-np.arange(half, dtype=np.float64) * 2.0 / ROT_DIM)
    ang = np.asarray(pos, np.float64)[:, None] * inv[None, :]
    cos, sin = np.cos(ang), np.sin(ang)
    n = ang.shape[0]
    rest = HEAD_DIM - ROT_DIM
    c = np.concatenate([cos, cos, np.ones((n, rest))], axis=1)
    t = np.concatenate([-sin, sin, np.zeros((n, rest))], axis=1)
    rep = LANES // HEAD_DIM
    return jnp.asarray(np.concatenate([np.tile(c, (1, rep)), np.tile(t, (1, rep))], axis=1), F32)


def _gate_weights(w_a, w_x):
    def dense(w):
        return jax.scipy.linalg.block_diag(*[w[n] for n in range(N_RNN_BLOCKS)])
    da, dx = dense(w_a), dense(w_x)
    half = D_RNN // 2
    chunks = [jnp.concatenate([da[c * half:(c + 1) * half, c * half:(c + 1) * half],
                               dx[c * half:(c + 1) * half, c * half:(c + 1) * half]], axis=1)
              for c in range(2)]
    return jnp.stack(chunks).astype(BF16)


def kernel(x_prompt, x_sample, cache_k, cache_v, state_conv, state_h, norm1_g, w_in, conv_w, conv_b,
           w_gate_a, b_gate_a, w_gate_x, b_gate_x, lru_lambda, w_out, norm2_g, w_ffn_gate, w_ffn_up,
           w_ffn_down, final_norm_g):
    bp, s, _ = x_prompt.shape
    nb, nt, _ = x_sample.shape
    depth = norm1_g.shape[0]
    w_buf = cache_k.shape[2]
    assert bp == 1 and s % ATT_BLOCK == 0 and w_buf == MAX_WINDOW and nt * nb == BAND

    xp = x_prompt.reshape(s, D_MODEL)
    xs = x_sample.transpose(1, 0, 2).reshape(nt * nb, D_MODEL)
    rope_p = _rope_tables(_class_major_positions(s, TM_FRONT))
    rope_s = _rope_tables(PAST_LEN + np.repeat(np.arange(nt), nb))
    ck_t = cache_k.transpose(0, 1, 3, 4, 2)
    cv_t = cache_v.transpose(0, 1, 3, 4, 2)
    cst = state_conv.transpose(0, 2, 1, 3)
    gf = final_norm_g.reshape(1, D_MODEL)
    row = lambda v: v.reshape(1, -1)

    pc, ph, sc, sh = [], [], [], []
    k_all = v_all = k_last = v_last = None
    w_in_b = w_in.astype(BF16)
    wo_b = w_out.astype(BF16)
    wg_b = w_ffn_gate.astype(BF16)
    wu_b = w_ffn_up.astype(BF16)
    wd_b = w_ffn_down.astype(BF16)
    for l in range(depth):
        gates = _gate_weights(w_gate_a[l], w_gate_x[l])
        common = (conv_w[l], row(conv_b[l]), gates, row(b_gate_a[l]), row(b_gate_x[l]),
                  row(lru_lambda[l]))
        final = l == depth - 1

        qkv1, qkv4, qkv16, rnn, k_last, v_last, tail, h_last = _front_call(
            xp, row(norm1_g[l]), w_in_b, rope_p, *common, k_last, v_last, layer=l)
        att = _attn_call(qkv1, qkv4, qkv16)
        xp, k_all, v_all = _ffn_call(xp, att, rnn, wo_b, row(norm2_g[l]), wg_b, wu_b, wd_b, gf,
                                     layer=l, final_norm=final, tm=TM_FFN,
                                     slide=(ck_t, cv_t, k_all, v_all, nt))
        pc.append(tail[:, SUBLANES - 1, :].reshape(1, CONV_WIDTH - 1, D_RNN))
        ph.append(h_last[0:1])

        q_s, k_s, v_s, rnn_s, nconv, hl = _sample_front_call(
            xs, row(norm1_g[l]), w_in_b, rope_s, *common, cst[l], state_h[l],
            layer=l, nb=nb, nt=nt)

        def split(v):
            return v.reshape(nt, nb, N_HEADS, HEAD_DIM)

        q_b = jnp.pad(split(q_s).transpose(1, 2, 0, 3), ((0, 0), (0, 0), (0, SUBLANES - nt), (0, 0)))

        def fresh(v):
            return jnp.pad(v.reshape(nt, nb, D_ATT).transpose(1, 0, 2),
                           ((0, 0), (0, SUBLANES - nt), (0, 0)))

        k_all, v_all, att_s = _sample_attn_call(
            q_b, ck_t, cv_t, fresh(k_s), fresh(v_s), k_all, v_all, layer=l, nt=nt)
        att_s = att_s[:, :, :nt].transpose(2, 0, 1, 3).reshape(nt * nb, D_ATT).astype(BF16)
        xs = _ffn_call(xs, att_s, rnn_s, wo_b, row(norm2_g[l]), wg_b, wu_b, wd_b, gf,
                       layer=l, final_norm=final, tm=nt * nb)
        sc.append(nconv.transpose(1, 0, 2))
        sh.append(hl)

    y_prompt = xp.reshape(1, s, D_MODEL)
    y_sample = xs.reshape(nt, nb, D_MODEL).transpose(1, 0, 2)
    sample_k = k_all.transpose(0, 1, 4, 2, 3)
    sample_v = v_all.transpose(0, 1, 4, 2, 3)
    window = (depth, 1, N_HEADS, HEAD_DIM, MAX_WINDOW)
    prompt_k = k_last.reshape(window).transpose(0, 1, 4, 2, 3)
    prompt_v = v_last.reshape(window).transpose(0, 1, 4, 2, 3)
    return (y_prompt, y_sample, prompt_k, prompt_v, jnp.stack(pc), jnp.stack(ph),
            sample_k, sample_v, jnp.stack(sc), jnp.stack(sh))
```
